```python
import jax, jax.numpy as jnp
from jax import lax
import numpy as np

D_MODEL = 4096
BATCH = 4
SEQ = 2048
DEPTH = 4
DEC_BATCH = 8
DEC_SEQ = 8
PAST_LEN = 8192
PAGE_SIZE = 128

HEAD_DIM = 128
ATTN_WIDTH = D_MODEL // 2
N_HEADS = ATTN_WIDTH // HEAD_DIM
CONV_DIM = D_MODEL - ATTN_WIDTH
CONV_GROUPS = CONV_DIM // HEAD_DIM
CONV_K = 3
MOBA_BLOCK = 256
MOBA_TOPK = 3
Q_CHUNK = 8
D_FF = 4 * D_MODEL
ADA_COLS = 6 * D_MODEL
IN_COLS = 3 * ATTN_WIDTH + 3 * CONV_DIM
ALIBI_MAX_BIAS = 8.0
EPS = 1e-6

kernel_name = 'hymba_moba_shortconv_adaln_step'


def rmsnorm(x, g):
    xf = x.astype(jnp.float32)
    xf = xf * lax.rsqrt(jnp.mean(xf * xf, axis=-1, keepdims=True) + EPS)
    return (xf * g.astype(jnp.float32)).astype(x.dtype)


def group_rmsnorm(y, g, n_groups):
    b, t, w = y.shape
    yf = y.astype(jnp.float32).reshape(b, t, n_groups, w // n_groups)
    yf = yf * lax.rsqrt(jnp.mean(yf * yf, axis=-1, keepdims=True) + EPS)
    return (yf.reshape(b, t, w) * g.astype(jnp.float32)).astype(y.dtype)


def alibi_slopes():
    return 2.0 ** (-ALIBI_MAX_BIAS * jnp.arange(1, N_HEADS + 1, dtype=jnp.float32) / N_HEADS)


def moba_attention(q, k, v, q_offset):
    bsz, tq, nh, dh = q.shape
    lk = k.shape[1]
    nb = -(-lk // MOBA_BLOCK)
    pad = nb * MOBA_BLOCK - lk
    padw = ((0, 0), (0, pad), (0, 0), (0, 0))
    kb = jnp.pad(k, padw).reshape(bsz, nb, MOBA_BLOCK, nh, dh).transpose(0, 3, 1, 2, 4)
    vb = jnp.pad(v, padw).reshape(bsz, nb, MOBA_BLOCK, nh, dh).transpose(0, 3, 1, 2, 4)
    kmean = jnp.mean(kb.astype(jnp.float32), axis=3)
    n_sel = min(MOBA_TOPK, nb)
    c = min(Q_CHUNK, tq)
    n_chunks = -(-tq // c)
    tq_pad = n_chunks * c
    q_chunks = jnp.pad(q, ((0, 0), (0, tq_pad - tq), (0, 0), (0, 0))).reshape(
        bsz, n_chunks, c, nh, dh).transpose(1, 0, 3, 2, 4)
    pos_chunks = (q_offset + jnp.arange(tq_pad, dtype=jnp.int32)).reshape(n_chunks, c)
    slopes = alibi_slopes()
    scale = HEAD_DIM ** -0.5
    blk = jnp.arange(MOBA_BLOCK, dtype=jnp.int32)
    bi = jnp.arange(bsz)[:, None, None, None]
    hi = jnp.arange(nh)[None, :, None, None]
    blocks = jnp.arange(nb, dtype=jnp.int32)

    def one_chunk(args):
        qc, pc = args
        own = pc[0] // MOBA_BLOCK
        gate = jnp.einsum('bhcd,bhnd->bhcn', qc.astype(jnp.float32), kmean)
        is_past = blocks[None, :] < (pc // MOBA_BLOCK)[:, None]
        gate = jnp.where(is_past, gate, -jnp.inf)
        gval, gidx = lax.top_k(gate, n_sel)
        sel_ok = jnp.isfinite(gval)[..., None]
        k_sel = kb[bi, hi, gidx]
        v_sel = vb[bi, hi, gidx]
        k_own = lax.dynamic_index_in_dim(kb, own, axis=2, keepdims=False)
        v_own = lax.dynamic_index_in_dim(vb, own, axis=2, keepdims=False)
        s_sel = jnp.einsum('bhcd,bhcnkd->bhcnk', qc, k_sel, preferred_element_type=jnp.float32) * scale
        s_own = jnp.einsum('bhcd,bhkd->bhck', qc, k_own, preferred_element_type=jnp.float32) * scale
        dist_sel = pc[None, None, :, None, None] - (gidx[..., None] * MOBA_BLOCK + blk)
        dist_own = pc[:, None] - (own * MOBA_BLOCK + blk)[None, :]
        s_sel = jnp.where(sel_ok, s_sel - slopes[None, :, None, None, None] * dist_sel, -jnp.inf)
        s_own = jnp.where(dist_own >= 0, s_own - slopes[None, :, None, None] * dist_own, -jnp.inf)
        scores = jnp.concatenate([s_sel.reshape(bsz, nh, c, n_sel * MOBA_BLOCK), s_own], axis=-1)
        p = jax.nn.softmax(scores, axis=-1).astype(v.dtype)
        p_sel = p[..., :n_sel * MOBA_BLOCK].reshape(bsz, nh, c, n_sel, MOBA_BLOCK)
        p_own = p[..., n_sel * MOBA_BLOCK:]
        return (jnp.einsum('bhcnk,bhcnkd->bhcd', p_sel, v_sel)
                + jnp.einsum('bhck,bhkd->bhcd', p_own, v_own))

    out = lax.map(one_chunk, (q_chunks, pos_chunks))
    return out.transpose(1, 0, 3, 2, 4).reshape(bsz, tq_pad, nh * dh)[:, :tq]


def short_conv(u, w, u_prev):
    t = u.shape[1]
    full = jnp.concatenate([u_prev, u], axis=1)
    y = sum(w[i] * full[:, i:i + t] for i in range(CONV_K))
    return y, full[:, t:]


def trunk_layer(x, c, k_past, v_past, u_prev, w_ada, b_ada, g1, g2, w_in, conv_w,
                g_ao, g_co, w_out, w_up, w_down):
    bsz, t, _ = x.shape
    mod = jax.nn.silu(c) @ w_ada + b_ada
    sh_a, sc_a, gt_a, sh_m, sc_m, gt_m = [m[:, None, :] for m in jnp.split(mod, 6, axis=-1)]
    h = rmsnorm(x, g1) * (1 + sc_a) + sh_a
    proj = h @ w_in
    a, cd = ATTN_WIDTH, CONV_DIM
    q, k, v, gb, gc, u_in = jnp.split(proj, [a, 2 * a, 3 * a, 3 * a + cd, 3 * a + 2 * cd], axis=-1)
    q = q.reshape(bsz, t, N_HEADS, HEAD_DIM)
    k = k.reshape(bsz, t, N_HEADS, HEAD_DIM)
    v = v.reshape(bsz, t, N_HEADS, HEAD_DIM)
    if k_past is None:
        k_all, v_all, offset = k, v, 0
    else:
        k_all = jnp.concatenate([k_past, k], axis=1)
        v_all = jnp.concatenate([v_past, v], axis=1)
        offset = k_past.shape[1]
    attn = moba_attention(q, k_all, v_all, offset)
    conv, u_new = short_conv(gc * u_in, conv_w, u_prev)
    conv_out = gb * conv
    mixed = jnp.concatenate([group_rmsnorm(attn, g_ao, N_HEADS),
                             group_rmsnorm(conv_out, g_co, CONV_GROUPS)], axis=-1)
    x = x + gt_a * (mixed @ w_out)
    h = rmsnorm(x, g2) * (1 + sc_m) + sh_m
    x = x + gt_m * (jnp.square(jax.nn.relu(h @ w_up)) @ w_down)
    return x, k, v, u_new


def setup_inputs(seed: int = 0) -> dict:
    key = jax.random.key(seed)
    ks = jax.random.split(key, 20)
    f32 = jnp.float32
    n_pages = PAST_LEN // PAGE_SIZE
    n_used = DEC_BATCH * n_pages
    n_phys = n_used + max(1, n_used // 4)

    def nrm(k, shape, s):
        return jax.random.normal(k, shape, f32) * s

    page_table = jax.random.permutation(ks[5], n_phys)[:n_used].reshape(DEC_BATCH, n_pages).astype(jnp.int32)
    return {
        'x_prompt': nrm(ks[0], (BATCH, SEQ, D_MODEL), 1.0),
        'x_sample': nrm(ks[1], (DEC_BATCH, DEC_SEQ, D_MODEL), 1.0),
        'cache_k': nrm(ks[2], (DEPTH, n_phys, PAGE_SIZE, N_HEADS, HEAD_DIM), 1.0),
        'cache_v': nrm(ks[3], (DEPTH, n_phys, PAGE_SIZE, N_HEADS, HEAD_DIM), 1.0),
        'state_conv': nrm(ks[4], (DEPTH, DEC_BATCH, CONV_K - 1, CONV_DIM), 1.0),
        'page_table': page_table,
        'c_prompt': nrm(ks[6], (BATCH, D_MODEL), 1.0),
        'c_sample': nrm(ks[7], (DEC_BATCH, D_MODEL), 1.0),
        'w_ada': nrm(ks[8], (DEPTH, D_MODEL, ADA_COLS), 0.5 * D_MODEL ** -0.5),
        'b_ada': nrm(ks[9], (DEPTH, ADA_COLS), 0.01),
        'g_norm1': 1.0 + nrm(ks[10], (DEPTH, D_MODEL), 0.01),
        'g_norm2': 1.0 + nrm(ks[11], (DEPTH, D_MODEL), 0.01),
        'w_in': nrm(ks[12], (DEPTH, D_MODEL, IN_COLS), D_MODEL ** -0.5),
        'conv_w': nrm(ks[13], (DEPTH, CONV_K, CONV_DIM), CONV_K ** -0.5),
        'g_attn_out': 1.0 + nrm(ks[14], (DEPTH, ATTN_WIDTH), 0.01),
        'g_conv_out': 1.0 + nrm(ks[15], (DEPTH, CONV_DIM), 0.01),
        'w_out': nrm(ks[16], (DEPTH, D_MODEL, D_MODEL), D_MODEL ** -0.5),
        'w_up': nrm(ks[17], (DEPTH, D_MODEL, D_FF), D_MODEL ** -0.5),
        'w_down': nrm(ks[18], (DEPTH, D_FF, D_MODEL), D_FF ** -0.5),
        'g_final': 1.0 + nrm(ks[19], (D_MODEL,), 0.01),
    }


def reference(x_prompt, x_sample, cache_k, cache_v, state_conv, page_table, c_prompt, c_sample,
              w_ada, b_ada, g_norm1, g_norm2, w_in, conv_w, g_attn_out, g_conv_out,
              w_out, w_up, w_down, g_final):
    dec_b, n_pages = page_table.shape
    xp, xs = x_prompt, x_sample
    u_zero = jnp.zeros((x_prompt.shape[0], CONV_K - 1, CONV_DIM), x_prompt.dtype)
    kp_l, vp_l, cp_l, ks_l, vs_l, cs_l = [], [], [], [], [], []
    for l in range(DEPTH):
        lw = (w_ada[l], b_ada[l], g_norm1[l], g_norm2[l], w_in[l], conv_w[l],
              g_attn_out[l], g_conv_out[l], w_out[l], w_up[l], w_down[l])
        xp, kp, vp, cp = trunk_layer(xp, c_prompt, None, None, u_zero, *lw)
        k_past = cache_k[l, page_table].reshape(dec_b, n_pages * PAGE_SIZE, N_HEADS, HEAD_DIM)
        v_past = cache_v[l, page_table].reshape(dec_b, n_pages * PAGE_SIZE, N_HEADS, HEAD_DIM)
        xs, kn, vn, cn = trunk_layer(xs, c_sample, k_past, v_past, state_conv[l], *lw)
        kp_l.append(kp); vp_l.append(vp); cp_l.append(cp)
        ks_l.append(kn); vs_l.append(vn); cs_l.append(cn)
    y_prompt = rmsnorm(xp, g_final)
    y_sample = rmsnorm(xs, g_final)
    return (y_prompt, y_sample, jnp.stack(kp_l), jnp.stack(vp_l), jnp.stack(cp_l),
            jnp.stack(ks_l), jnp.stack(vs_l), jnp.stack(cs_l))
```

```python
import functools

import jax
import jax.numpy as jnp
from jax import lax
from jax.experimental import pallas as pl
from jax.experimental.pallas import tpu as pltpu

F32 = jnp.float32
BF16 = jnp.bfloat16

HEAD_DIM = 128
PAGE_SIZE = 128
MOBA_BLOCK = 256
MOBA_TOPK = 3
ALIBI_MAX_BIAS = 8.0
EPS = 1e-6
PAGES_PER_BLOCK = MOBA_BLOCK // PAGE_SIZE

V7X_VMEM_BYTES = 64 * 2**20
VMEM_LIMIT_BYTES = 56 * 2**20
LANES = 128
SUBLANES = 8

NEG_INF = float("-inf")
_NT = (((1,), (1,)), ((), ()))
_TN = (((0,), (0,)), ((), ()))


def _params(*sem):
    return pltpu.CompilerParams(dimension_semantics=sem, vmem_limit_bytes=VMEM_LIMIT_BYTES)


def _ada_kernel(c_ref, w_ref, b_ref, o_ref):
    c = c_ref[...]
    s = c * jax.nn.sigmoid(c)
    o_ref[...] = jnp.dot(s.astype(BF16), w_ref[...].astype(BF16),
                         preferred_element_type=F32) + b_ref[...]


def _ada(c_all, w_ada, b_ada, tn=512):
    depth, d, cols = w_ada.shape
    rows = c_all.shape[0]
    return pl.pallas_call(
        _ada_kernel,
        grid=(depth, cols // tn),
        in_specs=[
            pl.BlockSpec((rows, d), lambda l, j: (0, 0)),
            pl.BlockSpec((None, d, tn), lambda l, j: (l, 0, j)),
            pl.BlockSpec((None, 1, tn), lambda l, j: (l, 0, j)),
        ],
        out_specs=pl.BlockSpec((None, rows, tn), lambda l, j: (l, 0, j)),
        out_shape=jax.ShapeDtypeStruct((depth, rows, cols), F32),
        compiler_params=_params("arbitrary", "arbitrary"),
        name="ada_mod",
    )(c_all, w_ada, b_ada.reshape(depth, 1, cols))


def _norm_mod_kernel(x_ref, g_ref, sc_ref, sh_ref, o_ref):
    x = x_ref[...]
    ms = jnp.mean(x * x, axis=-1, keepdims=True)
    xn = x * lax.rsqrt(ms + EPS) * g_ref[...]
    o_ref[...] = (xn * (1.0 + sc_ref[...]) + sh_ref[...]).astype(o_ref.dtype)


def _norm_kernel(x_ref, g_ref, o_ref):
    x = x_ref[...]
    ms = jnp.mean(x * x, axis=-1, keepdims=True)
    o_ref[...] = (x * lax.rsqrt(ms + EPS) * g_ref[...]).astype(o_ref.dtype)


def _row_tiles(nb_total, t_total, max_rows):
    if t_total >= max_rows:
        return 1, max_rows
    assert nb_total * t_total <= max_rows
    return nb_total, t_total


def _norm_mod(x, g, mod, shift_chunk, scale_chunk, max_rows=256):
    nbt, t, d = x.shape
    nb, tt = _row_tiles(nbt, t, max_rows)
    tpb = t // tt
    return pl.pallas_call(
        _norm_mod_kernel,
        grid=(nbt // nb, tpb),
        in_specs=[
            pl.BlockSpec((nb, tt, d), lambda b, i: (b, i, 0)),
            pl.BlockSpec((1, d), lambda b, i: (0, 0)),
            pl.BlockSpec((nb, 1, d), lambda b, i: (b, 0, scale_chunk)),
            pl.BlockSpec((nb, 1, d), lambda b, i: (b, 0, shift_chunk)),
        ],
        out_specs=pl.BlockSpec((nb, tt, d), lambda b, i: (b, i, 0)),
        out_shape=jax.ShapeDtypeStruct(x.shape, BF16),
        compiler_params=_params("arbitrary", "arbitrary"),
        name="norm_mod",
    )(x, g.reshape(1, d), mod, mod)


def _final_norm(x, g, max_rows=256):
    nbt, t, d = x.shape
    nb, tt = _row_tiles(nbt, t, max_rows)
    return pl.pallas_call(
        _norm_kernel,
        grid=(nbt // nb, t // tt),
        in_specs=[
            pl.BlockSpec((nb, tt, d), lambda b, i: (b, i, 0)),
            pl.BlockSpec((1, d), lambda b, i: (0, 0)),
        ],
        out_specs=pl.BlockSpec((nb, tt, d), lambda b, i: (b, i, 0)),
        out_shape=jax.ShapeDtypeStruct(x.shape, x.dtype),
        compiler_params=_params("arbitrary", "arbitrary"),
        name="final_norm",
    )(x, g.reshape(1, d))


def _mm_kernel(x_ref, w_ref, o_ref, *, relu2):
    acc = jnp.dot(x_ref[...], w_ref[...].astype(BF16), preferred_element_type=F32)
    if relu2:
        acc = jnp.square(jnp.maximum(acc, 0.0))
    o_ref[...] = acc.astype(o_ref.dtype)


def _mm(x2d, w, layer, col0, ncols, out_dtype, relu2=False, tm=1024, tn=512):
    m, k = x2d.shape
    tm = min(tm, m)
    off = col0 // tn
    return pl.pallas_call(
        functools.partial(_mm_kernel, relu2=relu2),
        grid=(m // tm, ncols // tn),
        in_specs=[
            pl.BlockSpec((tm, k), lambda i, j: (i, 0)),
            pl.BlockSpec((None, k, tn), lambda i, j: (layer, 0, off + j)),
        ],
        out_specs=pl.BlockSpec((tm, tn), lambda i, j: (i, j)),
        out_shape=jax.ShapeDtypeStruct((m, ncols), out_dtype),
        compiler_params=_params("arbitrary", "arbitrary"),
        name="mm",
    )(x2d, w)


def _mm_res2_kernel(a_ref, c_ref, wa_ref, wc_ref, x_ref, g_ref, o_ref):
    acc = jnp.dot(a_ref[...], wa_ref[...].astype(BF16), preferred_element_type=F32)
    acc = acc + jnp.dot(c_ref[...], wc_ref[...].astype(BF16), preferred_element_type=F32)
    o_ref[...] = x_ref[...] + g_ref[...] * acc.reshape(x_ref.shape)


def _out_proj(a2d, c2d, w_out, layer, x, mod, gate_chunk, max_rows=1024, tn=512):
    nbt, t, d = x.shape
    ka = a2d.shape[1]
    nb, tt = _row_tiles(nbt, t, max_rows)
    tpb = t // tt
    tm = nb * tt
    gate_off = gate_chunk * (d // tn)
    return pl.pallas_call(
        _mm_res2_kernel,
        grid=(a2d.shape[0] // tm, d // tn),
        in_specs=[
            pl.BlockSpec((tm, ka), lambda i, j: (i, 0)),
            pl.BlockSpec((tm, ka), lambda i, j: (i, 0)),
            pl.BlockSpec((None, ka, tn), lambda i, j: (layer, 0, j)),
            pl.BlockSpec((None, ka, tn), lambda i, j: (layer, 1, j)),
            pl.BlockSpec((nb, tt, tn), lambda i, j: (i // tpb, i % tpb, j)),
            pl.BlockSpec((nb, 1, tn), lambda i, j: (i // tpb, 0, gate_off + j)),
        ],
        out_specs=pl.BlockSpec((nb, tt, tn), lambda i, j: (i // tpb, i % tpb, j)),
        out_shape=jax.ShapeDtypeStruct(x.shape, x.dtype),
        compiler_params=_params("arbitrary", "arbitrary"),
        name="out_proj",
    )(a2d, c2d, w_out, w_out, x, mod)


def _mm_res_k_kernel(h_ref, w_ref, x_ref, g_ref, o_ref, acc_ref):
    k = pl.program_id(2)

    @pl.when(k == 0)
    def _():
        acc_ref[...] = jnp.zeros_like(acc_ref)

    acc_ref[...] += jnp.dot(h_ref[...], w_ref[...].astype(BF16), preferred_element_type=F32)

    @pl.when(k == pl.num_programs(2) - 1)
    def _():
        o_ref[...] = x_ref[...] + g_ref[...] * acc_ref[...].reshape(x_ref.shape)


def _down_proj(h2d, w_down, layer, x, mod, gate_chunk, max_rows=1024, tn=1024, tk=1024):
    nbt, t, d = x.shape
    kdim = h2d.shape[1]
    nb, tt = _row_tiles(nbt, t, max_rows)
    tpb = t // tt
    tm = nb * tt
    gate_off = gate_chunk * (d // tn)
    return pl.pallas_call(
        _mm_res_k_kernel,
        grid=(h2d.shape[0] // tm, d // tn, kdim // tk),
        in_specs=[
            pl.BlockSpec((tm, tk), lambda i, j, k: (i, k)),
            pl.BlockSpec((None, tk, tn), lambda i, j, k: (layer, k, j)),
            pl.BlockSpec((nb, tt, tn), lambda i, j, k: (i // tpb, i % tpb, j)),
            pl.BlockSpec((nb, 1, tn), lambda i, j, k: (i // tpb, 0, gate_off + j)),
        ],
        out_specs=pl.BlockSpec((nb, tt, tn), lambda i, j, k: (i // tpb, i % tpb, j)),
        out_shape=jax.ShapeDtypeStruct(x.shape, x.dtype),
        scratch_shapes=[pltpu.VMEM((tm, tn), F32)],
        compiler_params=_params("arbitrary", "arbitrary", "arbitrary"),
        name="down_proj",
    )(h2d, w_down, x, mod)


_PAD_ROWS = SUBLANES


def _conv_kernel(gb_ref, gc_ref, u_ref, prev_ref, w_ref, g_ref, o_ref, un_ref, pad_ref, *, t, cw, kw):
    u = gc_ref[0] * u_ref[0]
    pad_ref[pl.ds(_PAD_ROWS - (kw - 1), kw - 1), :] = prev_ref[0]
    pad_ref[pl.ds(_PAD_ROWS, t), :] = u
    w = w_ref[...]
    y = None
    for i in range(kw):
        tap = u if i == kw - 1 else pad_ref[pl.ds(_PAD_ROWS - (kw - 1) + i, t), :]
        term = w[i:i + 1, :] * tap
        y = term if y is None else y + term
    co = gb_ref[0] * y
    outs = []
    for gi in range(cw // HEAD_DIM):
        sl = slice(gi * HEAD_DIM, (gi + 1) * HEAD_DIM)
        c = co[:, sl]
        ms = jnp.mean(c * c, axis=-1, keepdims=True)
        outs.append(c * lax.rsqrt(ms + EPS) * g_ref[:, sl])
    o_ref[0] = jnp.concatenate(outs, axis=-1).astype(o_ref.dtype)
    un_ref[0] = pad_ref[pl.ds(_PAD_ROWS + t - (kw - 1), kw - 1), :]


def _short_conv(cv, prev, conv_w_l, g_co_l, cw=256):
    nb, t, c3 = cv.shape
    c = c3 // 3
    kw = conv_w_l.shape[0]
    ncb = c // cw
    return pl.pallas_call(
        functools.partial(_conv_kernel, t=t, cw=cw, kw=kw),
        grid=(nb, ncb),
        in_specs=[
            pl.BlockSpec((1, t, cw), lambda b, j: (b, 0, j)),
            pl.BlockSpec((1, t, cw), lambda b, j: (b, 0, ncb + j)),
            pl.BlockSpec((1, t, cw), lambda b, j: (b, 0, 2 * ncb + j)),
            pl.BlockSpec((1, kw - 1, cw), lambda b, j: (b, 0, j)),
            pl.BlockSpec((kw, cw), lambda b, j: (0, j)),
            pl.BlockSpec((1, cw), lambda b, j: (0, j)),
        ],
        out_specs=[
            pl.BlockSpec((1, t, cw), lambda b, j: (b, 0, j)),
            pl.BlockSpec((1, kw - 1, cw), lambda b, j: (b, 0, j)),
        ],
        out_shape=[
            jax.ShapeDtypeStruct((nb, t, c), BF16),
            jax.ShapeDtypeStruct((nb, kw - 1, c), F32),
        ],
        scratch_shapes=[pltpu.VMEM((t + _PAD_ROWS, cw), F32)],
        compiler_params=_params("arbitrary", "arbitrary"),
        name="short_conv",
    )(cv, cv, cv, prev, conv_w_l, g_co_l.reshape(1, c))


def _topk_select(gate, n_blocks):
    bidx = lax.broadcasted_iota(jnp.int32, gate.shape, 0)
    sel = jnp.zeros(gate.shape, F32)
    for n in range(n_blocks):
        row = gate[n:n + 1, :]
        beats = (gate > row) | ((gate == row) & (bidx < n))
        cnt = jnp.sum(beats.astype(F32), axis=0, keepdims=True)
        keep = jnp.where((cnt < MOBA_TOPK) & (row > NEG_INF), 1.0, 0.0)
        sel = jnp.where(bidx == n, keep, sel)
    return sel


def _attn_prompt_kernel(q_ref, k_ref, v_ref, g_ref, slope_ref, o_ref,
                        kb_ref, vb_ref, kmean_ref, sel_ref, *, n_blocks, scale):
    qi = pl.program_id(2)
    blk = MOBA_BLOCK

    @pl.when(qi == 0)
    def _():
        k = k_ref[0]
        kb_ref[...] = k.astype(BF16)
        vb_ref[...] = v_ref[0].astype(BF16)
        kmean_ref[...] = jnp.mean(k.reshape(n_blocks, blk, HEAD_DIM), axis=1)

    q = q_ref[0]
    gate = lax.dot_general(kmean_ref[...], q, _NT, precision=lax.Precision.HIGHEST,
                           preferred_element_type=F32)
    bidx = lax.broadcasted_iota(jnp.int32, gate.shape, 0)
    gate = jnp.where(bidx < qi, gate, NEG_INF)
    sel_ref[...] = _topk_select(gate, n_blocks)

    qb = q.astype(BF16)
    slope = slope_ref[0]
    base = (lax.broadcasted_iota(jnp.int32, (blk, blk), 1)
            - lax.broadcasted_iota(jnp.int32, (blk, blk), 0))
    basef = base.astype(F32)

    def block_scores(start):
        kn = kb_ref[pl.ds(start, blk), :]
        return lax.dot_general(kn, qb, _NT, preferred_element_type=F32) * scale

    def block_pv(start, p):
        vn = vb_ref[pl.ds(start, blk), :]
        return lax.dot_general(vn, p.astype(BF16), _TN, preferred_element_type=F32)

    own = pl.multiple_of(qi * blk, blk)
    s = block_scores(own) - slope * basef
    s = jnp.where(base >= 0, s, NEG_INF)
    m = jnp.max(s, axis=0, keepdims=True)
    p = jnp.exp(s - m)
    l = jnp.sum(p, axis=0, keepdims=True)
    acc = block_pv(own, p)

    def body(n, carry):
        m, l, acc = carry
        start = pl.multiple_of(n * blk, blk)
        dist = basef + ((qi - n) * blk).astype(F32)
        s = block_scores(start) - slope * dist
        s = jnp.where(sel_ref[pl.ds(n, 1), :] > 0.0, s, NEG_INF)
        m_new = jnp.maximum(m, jnp.max(s, axis=0, keepdims=True))
        alpha = jnp.exp(m - m_new)
        p = jnp.exp(s - m_new)
        l = alpha * l + jnp.sum(p, axis=0, keepdims=True)
        acc = alpha * acc + block_pv(start, p)
        return m_new, l, acc

    m, l, acc = lax.fori_loop(0, qi, body, (m, l, acc))
    o = (acc / l).T
    ms = jnp.mean(o * o, axis=-1, keepdims=True)
    o_ref[0] = (o * lax.rsqrt(ms + EPS) * g_ref[...]).astype(o_ref.dtype)


def _alibi_slopes(n_heads):
    return 2.0 ** (-ALIBI_MAX_BIAS * jnp.arange(1, n_heads + 1, dtype=F32) / n_heads)


def _attn_prompt(q, k, v, g_ao_l):
    nb, t, width = q.shape
    n_heads = width // HEAD_DIM
    n_blocks = t // MOBA_BLOCK
    slopes = jnp.broadcast_to(_alibi_slopes(n_heads)[:, None, None], (n_heads, 1, MOBA_BLOCK))
    kern = functools.partial(_attn_prompt_kernel, n_blocks=n_blocks, scale=HEAD_DIM ** -0.5)
    return pl.pallas_call(
        kern,
        grid=(nb, n_heads, n_blocks),
        in_specs=[
            pl.BlockSpec((1, MOBA_BLOCK, HEAD_DIM), lambda b, h, i: (b, i, h)),
            pl.BlockSpec((1, t, HEAD_DIM), lambda b, h, i: (b, 0, h)),
            pl.BlockSpec((1, t, HEAD_DIM), lambda b, h, i: (b, 0, h)),
            pl.BlockSpec((1, HEAD_DIM), lambda b, h, i: (0, h)),
            pl.BlockSpec((1, 1, MOBA_BLOCK), lambda b, h, i: (h, 0, 0)),
        ],
        out_specs=pl.BlockSpec((1, MOBA_BLOCK, HEAD_DIM), lambda b, h, i: (b, i, h)),
        out_shape=jax.ShapeDtypeStruct((nb, t, width), BF16),
        scratch_shapes=[
            pltpu.VMEM((t, HEAD_DIM), BF16),
            pltpu.VMEM((t, HEAD_DIM), BF16),
            pltpu.VMEM((n_blocks, HEAD_DIM), F32),
            pltpu.VMEM((n_blocks, MOBA_BLOCK), F32),
        ],
        compiler_params=_params("arbitrary", "arbitrary", "arbitrary"),
        name="attn_prompt",
    )(q, k, v, g_ao_l.reshape(1, width), slopes)


def _diag_blocks(full, n_heads, rows_per_head):
    return jnp.concatenate(
        [full[h * rows_per_head:(h + 1) * rows_per_head, h * HEAD_DIM:(h + 1) * HEAD_DIM]
         for h in range(n_heads)], axis=0)


def _attn_sample_kernel(pt_ref, q_ref, kn_ref, vn_ref, k0_ref, k1_ref, v0_ref, v1_ref,
                        g_ref, slope_ref, qpos_ref, o_ref,
                        qf_ref, qb_ref, ksum_ref, m_ref, l_ref, op_ref, mo_ref, lo_ref, oo_ref,
                        *, n_heads, tq, n_blocks, scale):
    del pt_ref
    n = pl.program_id(1)
    cols = n_heads * tq
    width = n_heads * HEAD_DIM
    slope = slope_ref[...]
    qpos = qpos_ref[...]

    def attend(k_pages, v_pages, key_pos, causal):
        qb = qb_ref[...]
        s = jnp.concatenate(
            [lax.dot_general(kp.astype(BF16), qb, _NT, preferred_element_type=F32) for kp in k_pages],
            axis=0) * scale
        dist = qpos - key_pos
        s = s - slope * dist
        if causal:
            s = jnp.where(dist >= 0.0, s, NEG_INF)
        m = jnp.max(s, axis=0, keepdims=True)
        p_t = jnp.exp(s - m)
        l = jnp.sum(p_t, axis=0, keepdims=True)
        pb = p_t.T.astype(BF16)
        full = None
        for i, vp in enumerate(v_pages):
            part = jnp.dot(pb[:, i * PAGE_SIZE:(i + 1) * PAGE_SIZE], vp.astype(BF16),
                           preferred_element_type=F32)
            full = part if full is None else full + part
        return m, l, _diag_blocks(full, n_heads, tq).T

    @pl.when(n == 0)
    def _():
        q = q_ref[0]
        qt = jnp.concatenate([q] * n_heads, axis=0)
        r = lax.broadcasted_iota(jnp.int32, (cols, width), 0) // tq
        c = lax.broadcasted_iota(jnp.int32, (cols, width), 1) // HEAD_DIM
        qbd = jnp.where(r == c, qt, 0.0)
        qf_ref[...] = qbd
        qb_ref[...] = qbd.astype(BF16)
        kpos = qpos_ref[:, 0:1] + lax.broadcasted_iota(jnp.int32, (PAGE_SIZE, cols), 0).astype(F32)
        m, l, o_t = attend([kn_ref[0]], [vn_ref[0]], kpos, True)
        mo_ref[...] = m
        lo_ref[...] = l
        oo_ref[...] = o_t

    k0 = k0_ref[...]
    k1 = k1_ref[...]
    kpos = (n * MOBA_BLOCK + lax.broadcasted_iota(jnp.int32, (MOBA_BLOCK, cols), 0)).astype(F32)
    m, l, o_t = attend([k0, k1], [v0_ref[...], v1_ref[...]], kpos, False)
    m_ref[pl.ds(n, 1), :] = m
    l_ref[pl.ds(n, 1), :] = l
    op_ref[n] = o_t
    ksum_ref[pl.ds(n, 1), :] = (jnp.sum(k0, axis=0, keepdims=True)
                                + jnp.sum(k1, axis=0, keepdims=True))

    @pl.when(n == n_blocks - 1)
    def _():
        kmean = ksum_ref[...] * (1.0 / MOBA_BLOCK)
        gate = lax.dot_general(kmean, qf_ref[...], _NT, precision=lax.Precision.HIGHEST,
                               preferred_element_type=F32)
        sel = _topk_select(gate, n_blocks) > 0.0
        m_all = jnp.where(sel, m_ref[...], NEG_INF)
        mo = mo_ref[...]
        m_tot = jnp.maximum(jnp.max(m_all, axis=0, keepdims=True), mo)
        w = jnp.exp(m_all - m_tot)
        wo = jnp.exp(mo - m_tot)
        l_tot = jnp.sum(w * l_ref[...], axis=0, keepdims=True) + wo * lo_ref[...]
        acc = wo * oo_ref[...]
        for i in range(n_blocks):
            acc = acc + w[i:i + 1, :] * op_ref[i]
        o = (acc / l_tot).T
        ms = jnp.mean(o * o, axis=-1, keepdims=True)
        on = o * lax.rsqrt(ms + EPS)
        out = jnp.concatenate(
            [on[h * tq:(h + 1) * tq, :] * g_ref[:, h * HEAD_DIM:(h + 1) * HEAD_DIM]
             for h in range(n_heads)], axis=1)
        o_ref[0] = out.astype(o_ref.dtype)


def _attn_sample(q, k_new, v_new, cache_k, cache_v, layer, page_table, g_ao_l):
    nb, tq, width = q.shape
    n_heads = width // HEAD_DIM
    cols = n_heads * tq
    assert cols == LANES and tq <= PAGE_SIZE
    n_pages = page_table.shape[1]
    past_len = n_pages * PAGE_SIZE
    n_blocks = n_pages // PAGES_PER_BLOCK
    assert past_len % MOBA_BLOCK == 0 and (past_len + tq - 1) // MOBA_BLOCK == n_blocks
    pad = ((0, 0), (0, PAGE_SIZE - tq), (0, 0))
    kn = jnp.pad(k_new, pad)
    vn = jnp.pad(v_new, pad)
    col = jnp.arange(cols)
    slope_row = _alibi_slopes(n_heads)[col // tq].reshape(1, cols)
    qpos_row = (past_len + col % tq).astype(F32).reshape(1, cols)
    pt_flat = page_table.reshape(-1)

    def page_spec(which):
        return pl.BlockSpec(
            (None, None, PAGE_SIZE, width),
            lambda b, n, pt: (layer, pt[b * n_pages + PAGES_PER_BLOCK * n + which], 0, 0))

    row_spec = pl.BlockSpec((1, cols), lambda b, n, pt: (0, 0))
    grid_spec = pltpu.PrefetchScalarGridSpec(
        num_scalar_prefetch=1,
        grid=(nb, n_blocks),
        in_specs=[
            pl.BlockSpec((1, tq, width), lambda b, n, pt: (b, 0, 0)),
            pl.BlockSpec((1, PAGE_SIZE, width), lambda b, n, pt: (b, 0, 0)),
            pl.BlockSpec((1, PAGE_SIZE, width), lambda b, n, pt: (b, 0, 0)),
            page_spec(0), page_spec(1), page_spec(0), page_spec(1),
            pl.BlockSpec((1, width), lambda b, n, pt: (0, 0)),
            row_spec, row_spec,
        ],
        out_specs=pl.BlockSpec((1, tq, width), lambda b, n, pt: (b, 0, 0)),
        scratch_shapes=[
            pltpu.VMEM((cols, width), F32),
            pltpu.VMEM((cols, width), BF16),
            pltpu.VMEM((n_blocks, width), F32),
            pltpu.VMEM((n_blocks, cols), F32),
            pltpu.VMEM((n_blocks, cols), F32),
            pltpu.VMEM((n_blocks, HEAD_DIM, cols), F32),
            pltpu.VMEM((1, cols), F32),
            pltpu.VMEM((1, cols), F32),
            pltpu.VMEM((HEAD_DIM, cols), F32),
        ],
    )
    kern = functools.partial(_attn_sample_kernel, n_heads=n_heads, tq=tq, n_blocks=n_blocks,
                             scale=HEAD_DIM ** -0.5)
    return pl.pallas_call(
        kern,
        grid_spec=grid_spec,
        out_shape=jax.ShapeDtypeStruct((nb, tq, width), BF16),
        compiler_params=_params("arbitrary", "arbitrary"),
        name="attn_sample",
    )(pt_flat, q, kn, vn, cache_k, cache_k, cache_v, cache_v, g_ao_l.reshape(1, width),
      slope_row, qpos_row)


def _layer(x, mod, layer, prev, attend, g1, g2, w_in, conv_w, g_ao, g_co, w_out, w_up, w_down):
    nb, t, d = x.shape
    m = nb * t
    aw = g_ao.shape[1]
    cd = g_co.shape[1]
    h = _norm_mod(x, g1[layer], mod, shift_chunk=0, scale_chunk=1).reshape(m, d)
    q = _mm(h, w_in, layer, 0, aw, F32).reshape(nb, t, aw)
    k = _mm(h, w_in, layer, aw, aw, F32).reshape(nb, t, aw)
    v = _mm(h, w_in, layer, 2 * aw, aw, F32).reshape(nb, t, aw)
    cv = _mm(h, w_in, layer, 3 * aw, 3 * cd, F32).reshape(nb, t, 3 * cd)
    attn = attend(q, k, v, g_ao[layer])
    conv, u_new = _short_conv(cv, prev, conv_w[layer], g_co[layer])
    x = _out_proj(attn.reshape(m, aw), conv.reshape(m, cd), w_out, layer, x, mod, gate_chunk=2)
    h2 = _norm_mod(x, g2[layer], mod, shift_chunk=3, scale_chunk=4).reshape(m, d)
    hid = _mm(h2, w_up, layer, 0, w_up.shape[2], BF16, relu2=True)
    x = _down_proj(hid, w_down, layer, x, mod, gate_chunk=5)
    return x, k, v, u_new


def kernel(x_prompt, x_sample, cache_k, cache_v, state_conv, page_table, c_prompt, c_sample,
           w_ada, b_ada, g_norm1, g_norm2, w_in, conv_w, g_attn_out, g_conv_out,
           w_out, w_up, w_down, g_final):
    depth = w_ada.shape[0]
    bp, tp, d = x_prompt.shape
    bs, ts, _ = x_sample.shape
    aw = g_attn_out.shape[1]
    cd = g_conv_out.shape[1]
    n_heads = aw // HEAD_DIM
    kw = conv_w.shape[1]

    n_c = bp + bs
    c_rows = -(-n_c // SUBLANES) * SUBLANES
    c_all = jnp.concatenate([c_prompt, c_sample, jnp.zeros((c_rows - n_c, d), F32)], axis=0)
    mod = _ada(c_all, w_ada, b_ada)

    ck = cache_k.reshape(cache_k.shape[0], cache_k.shape[1], PAGE_SIZE, aw)
    cvv = cache_v.reshape(cache_v.shape[0], cache_v.shape[1], PAGE_SIZE, aw)
    u_zero = jnp.zeros((bp, kw - 1, cd), F32)

    weights = (g_norm1, g_norm2, w_in, conv_w, g_attn_out, g_conv_out, w_out, w_up, w_down)
    xp, xs = x_prompt, x_sample
    kp_l, vp_l, cp_l, ks_l, vs_l, cs_l = [], [], [], [], [], []
    for layer in range(depth):
        mod_p = mod[layer, :bp].reshape(bp, 1, -1)
        mod_s = mod[layer, bp:n_c].reshape(bs, 1, -1)
        xp, kp, vp, cp = _layer(xp, mod_p, layer, u_zero, _attn_prompt, *weights)
        attend_s = functools.partial(_attn_sample_wrapper, cache_k=ck, cache_v=cvv, layer=layer,
                                     page_table=page_table)
        xs, kn, vn, cn = _layer(xs, mod_s, layer, state_conv[layer], attend_s, *weights)
        kp_l.append(kp.reshape(bp, tp, n_heads, HEAD_DIM))
        vp_l.append(vp.reshape(bp, tp, n_heads, HEAD_DIM))
        cp_l.append(cp)
        ks_l.append(kn.reshape(bs, ts, n_heads, HEAD_DIM))
        vs_l.append(vn.reshape(bs, ts, n_heads, HEAD_DIM))
        cs_l.append(cn)
    y_prompt = _final_norm(xp, g_final)
    y_sample = _final_norm(xs, g_final)
    return (y_prompt, y_sample, jnp.stack(kp_l), jnp.stack(vp_l), jnp.stack(cp_l),
            jnp.stack(ks_l), jnp.stack(vs_l), jnp.stack(cs_l))


def _attn_sample_wrapper(q, k, v, g_ao_l, *, cache_k, cache_v, layer, page_table):
    return _attn_sample(q, k, v, cache_k, cache_v, layer, page_table, g_ao_l)
```

```python
import functools

import jax
import jax.numpy as jnp
from jax import lax
from jax.experimental import pallas as pl
from jax.experimental.pallas import tpu as pltpu

F32 = jnp.float32
BF16 = jnp.bfloat16

HEAD_DIM = 128
PAGE_SIZE = 128
MOBA_BLOCK = 256
MOBA_TOPK = 3
ALIBI_MAX_BIAS = 8.0
EPS = 1e-6
LOG2_E = 1.4426950408889634
PAGES_PER_BLOCK = MOBA_BLOCK // PAGE_SIZE

V7X_VMEM_BYTES = 64 * 2**20
VMEM_LIMIT_BYTES = V7X_VMEM_BYTES - 8 * 2**20
LANES = 128
SUBLANES = 8

NEG_INF = float("-inf")
_NT = (((1,), (1,)), ((), ()))


def _params(*sem):
    return pltpu.CompilerParams(dimension_semantics=sem, vmem_limit_bytes=VMEM_LIMIT_BYTES)


def _ada_kernel(c_ref, w_ref, b_ref, o_ref):
    c = c_ref[...]
    s = c * jax.nn.sigmoid(c)
    o_ref[...] = jnp.dot(s.astype(BF16), w_ref[...].astype(BF16),
                         preferred_element_type=F32) + b_ref[...]


def _ada(c_all, w_ada, b_ada, tn=512):
    depth, d, cols = w_ada.shape
    rows = c_all.shape[0]
    return pl.pallas_call(
        _ada_kernel,
        grid=(depth, cols // tn),
        in_specs=[
            pl.BlockSpec((rows, d), lambda l, j: (0, 0)),
            pl.BlockSpec((None, d, tn), lambda l, j: (l, 0, j)),
            pl.BlockSpec((None, 1, tn), lambda l, j: (l, 0, j)),
        ],
        out_specs=pl.BlockSpec((None, rows, tn), lambda l, j: (l, 0, j)),
        out_shape=jax.ShapeDtypeStruct((depth, rows, cols), F32),
        compiler_params=_params("arbitrary", "arbitrary"),
        name="ada_mod",
    )(c_all, w_ada, b_ada.reshape(depth, 1, cols))


def _norm_mod_kernel(x_ref, g_ref, sc_ref, sh_ref, o_ref):
    x = x_ref[...]
    ms = jnp.mean(x * x, axis=-1, keepdims=True)
    xn = x * lax.rsqrt(ms + EPS) * g_ref[...]
    o_ref[...] = (xn * (1.0 + sc_ref[...]) + sh_ref[...]).astype(o_ref.dtype)


def _norm_kernel(x_ref, g_ref, o_ref):
    x = x_ref[...]
    ms = jnp.mean(x * x, axis=-1, keepdims=True)
    o_ref[...] = (x * lax.rsqrt(ms + EPS) * g_ref[...]).astype(o_ref.dtype)


def _row_tiles(nb_total, t_total, max_rows):
    if t_total >= max_rows:
        return 1, max_rows
    assert nb_total * t_total <= max_rows
    return nb_total, t_total


def _norm_mod(x, g, mod, shift_chunk, scale_chunk, max_rows=256):
    nbt, t, d = x.shape
    nb, tt = _row_tiles(nbt, t, max_rows)
    tpb = t // tt
    return pl.pallas_call(
        _norm_mod_kernel,
        grid=(nbt // nb, tpb),
        in_specs=[
            pl.BlockSpec((nb, tt, d), lambda b, i: (b, i, 0)),
            pl.BlockSpec((1, d), lambda b, i: (0, 0)),
            pl.BlockSpec((nb, 1, d), lambda b, i: (b, 0, scale_chunk)),
            pl.BlockSpec((nb, 1, d), lambda b, i: (b, 0, shift_chunk)),
        ],
        out_specs=pl.BlockSpec((nb, tt, d), lambda b, i: (b, i, 0)),
        out_shape=jax.ShapeDtypeStruct(x.shape, BF16),
        compiler_params=_params("arbitrary", "arbitrary"),
        name="norm_mod",
    )(x, g.reshape(1, d), mod, mod)


def _final_norm(x, g, max_rows=256):
    nbt, t, d = x.shape
    nb, tt = _row_tiles(nbt, t, max_rows)
    return pl.pallas_call(
        _norm_kernel,
        grid=(nbt // nb, t // tt),
        in_specs=[
            pl.BlockSpec((nb, tt, d), lambda b, i: (b, i, 0)),
            pl.BlockSpec((1, d), lambda b, i: (0, 0)),
        ],
        out_specs=pl.BlockSpec((nb, tt, d), lambda b, i: (b, i, 0)),
        out_shape=jax.ShapeDtypeStruct(x.shape, x.dtype),
        compiler_params=_params("arbitrary", "arbitrary"),
        name="final_norm",
    )(x, g.reshape(1, d))


def _mm_kernel(x_ref, w_ref, o_ref, *, relu2):
    acc = jnp.dot(x_ref[...], w_ref[...].astype(BF16), preferred_element_type=F32)
    if relu2:
        acc = jnp.square(jnp.maximum(acc, 0.0))
    o_ref[...] = acc.astype(o_ref.dtype)


def _mm_tiles(m):
    return (2048, 256) if m >= 2048 else (m, 512)


def _mm(x2d, w, layer, col0, ncols, out_dtype, relu2=False):
    m, k = x2d.shape
    tm, tn = _mm_tiles(m)
    off = col0 // tn
    return pl.pallas_call(
        functools.partial(_mm_kernel, relu2=relu2),
        grid=(m // tm, ncols // tn),
        in_specs=[
            pl.BlockSpec((tm, k), lambda i, j: (i, 0)),
            pl.BlockSpec((None, k, tn), lambda i, j: (layer, 0, off + j)),
        ],
        out_specs=pl.BlockSpec((tm, tn), lambda i, j: (i, j)),
        out_shape=jax.ShapeDtypeStruct((m, ncols), out_dtype),
        compiler_params=_params("arbitrary", "arbitrary"),
        name="mm",
    )(x2d, w)


def _mm_res2_kernel(a_ref, c_ref, wa_ref, wc_ref, x_ref, g_ref, o_ref):
    acc = jnp.dot(a_ref[...], wa_ref[...].astype(BF16), preferred_element_type=F32)
    acc = acc + jnp.dot(c_ref[...], wc_ref[...].astype(BF16), preferred_element_type=F32)
    o_ref[...] = x_ref[...] + g_ref[...] * acc.reshape(x_ref.shape)


def _out_proj(a2d, c2d, w_out, layer, x, mod, gate_chunk):
    nbt, t, d = x.shape
    ka = a2d.shape[1]
    max_rows, tn = _mm_tiles(nbt * t)
    nb, tt = _row_tiles(nbt, t, max_rows)
    tpb = t // tt
    tm = nb * tt
    gate_off = gate_chunk * (d // tn)
    return pl.pallas_call(
        _mm_res2_kernel,
        grid=(a2d.shape[0] // tm, d // tn),
        in_specs=[
            pl.BlockSpec((tm, ka), lambda i, j: (i, 0), pipeline_mode=pl.Buffered(1)),
            pl.BlockSpec((tm, ka), lambda i, j: (i, 0), pipeline_mode=pl.Buffered(1)),
            pl.BlockSpec((None, ka, tn), lambda i, j: (layer, 0, j)),
            pl.BlockSpec((None, ka, tn), lambda i, j: (layer, 1, j)),
            pl.BlockSpec((nb, tt, tn), lambda i, j: (i // tpb, i % tpb, j)),
            pl.BlockSpec((nb, 1, tn), lambda i, j: (i // tpb, 0, gate_off + j)),
        ],
        out_specs=pl.BlockSpec((nb, tt, tn), lambda i, j: (i // tpb, i % tpb, j)),
        out_shape=jax.ShapeDtypeStruct(x.shape, x.dtype),
        compiler_params=_params("arbitrary", "arbitrary"),
        name="out_proj",
    )(a2d, c2d, w_out, w_out, x, mod)


def _mm_res_k_kernel(h_ref, w_ref, x_ref, g_ref, o_ref):
    k = pl.program_id(2)
    part = jnp.dot(h_ref[...], w_ref[...].astype(BF16),
                   preferred_element_type=F32).reshape(o_ref.shape)

    @pl.when(k == 0)
    def _():
        o_ref[...] = part

    @pl.when(k > 0)
    def _():
        o_ref[...] += part

    @pl.when(k == pl.num_programs(2) - 1)
    def _():
        o_ref[...] = x_ref[...] + g_ref[...] * o_ref[...]


def _down_proj(h2d, w_down, layer, x, mod, gate_chunk, tn=1024, tk=1024):
    nbt, t, d = x.shape
    kdim = h2d.shape[1]
    max_rows, _ = _mm_tiles(nbt * t)
    nb, tt = _row_tiles(nbt, t, max_rows)
    tpb = t // tt
    tm = nb * tt
    gate_off = gate_chunk * (d // tn)
    return pl.pallas_call(
        _mm_res_k_kernel,
        grid=(h2d.shape[0] // tm, d // tn, kdim // tk),
        in_specs=[
            pl.BlockSpec((tm, tk), lambda i, j, k: (i, k)),
            pl.BlockSpec((None, tk, tn), lambda i, j, k: (layer, k, j)),
            pl.BlockSpec((nb, tt, tn), lambda i, j, k: (i // tpb, i % tpb, j),
                         pipeline_mode=pl.Buffered(1)),
            pl.BlockSpec((nb, 1, tn), lambda i, j, k: (i // tpb, 0, gate_off + j)),
        ],
        out_specs=pl.BlockSpec((nb, tt, tn), lambda i, j, k: (i // tpb, i % tpb, j)),
        out_shape=jax.ShapeDtypeStruct(x.shape, x.dtype),
        compiler_params=_params("arbitrary", "arbitrary", "arbitrary"),
        name="down_proj",
    )(h2d, w_down, x, mod)


_PAD_ROWS = SUBLANES


def _conv_kernel(gb_ref, gc_ref, u_ref, prev_ref, w_ref, g_ref, o_ref, un_ref, pad_ref, *, t, cw, kw):
    u = gc_ref[0] * u_ref[0]
    pad_ref[pl.ds(_PAD_ROWS - (kw - 1), kw - 1), :] = prev_ref[0]
    pad_ref[pl.ds(_PAD_ROWS, t), :] = u
    w = w_ref[...]
    y = None
    for i in range(kw):
        tap = u if i == kw - 1 else pad_ref[pl.ds(_PAD_ROWS - (kw - 1) + i, t), :]
        term = w[i:i + 1, :] * tap
        y = term if y is None else y + term
    co = gb_ref[0] * y
    outs = []
    for gi in range(cw // HEAD_DIM):
        sl = slice(gi * HEAD_DIM, (gi + 1) * HEAD_DIM)
        c = co[:, sl]
        ms = jnp.mean(c * c, axis=-1, keepdims=True)
        outs.append(c * lax.rsqrt(ms + EPS) * g_ref[:, sl])
    o_ref[0] = jnp.concatenate(outs, axis=-1).astype(o_ref.dtype)
    un_ref[0] = pad_ref[pl.ds(_PAD_ROWS + t - (kw - 1), kw - 1), :]


def _short_conv(cv, prev, conv_w_l, g_co_l, cw=256):
    nb, t, c3 = cv.shape
    c = c3 // 3
    kw = conv_w_l.shape[0]
    ncb = c // cw
    return pl.pallas_call(
        functools.partial(_conv_kernel, t=t, cw=cw, kw=kw),
        grid=(nb, ncb),
        in_specs=[
            pl.BlockSpec((1, t, cw), lambda b, j: (b, 0, j)),
            pl.BlockSpec((1, t, cw), lambda b, j: (b, 0, ncb + j)),
            pl.BlockSpec((1, t, cw), lambda b, j: (b, 0, 2 * ncb + j)),
            pl.BlockSpec((1, kw - 1, cw), lambda b, j: (b, 0, j)),
            pl.BlockSpec((kw, cw), lambda b, j: (0, j)),
            pl.BlockSpec((1, cw), lambda b, j: (0, j)),
        ],
        out_specs=[
            pl.BlockSpec((1, t, cw), lambda b, j: (b, 0, j)),
            pl.BlockSpec((1, kw - 1, cw), lambda b, j: (b, 0, j)),
        ],
        out_shape=[
            jax.ShapeDtypeStruct((nb, t, c), BF16),
            jax.ShapeDtypeStruct((nb, kw - 1, c), F32),
        ],
        scratch_shapes=[pltpu.VMEM((t + _PAD_ROWS, cw), F32)],
        compiler_params=_params("arbitrary", "arbitrary"),
        name="short_conv",
    )(cv, cv, cv, prev, conv_w_l, g_co_l.reshape(1, c))


def _topk_select(gate, n_blocks):
    bidx = lax.broadcasted_iota(jnp.int32, gate.shape, 0)
    sel = jnp.zeros(gate.shape, F32)
    for n in range(n_blocks):
        row = gate[n:n + 1, :]
        beats = (gate > row) | ((gate == row) & (bidx < n))
        cnt = jnp.sum(beats.astype(F32), axis=0, keepdims=True)
        keep = jnp.where((cnt < MOBA_TOPK) & (row > NEG_INF), 1.0, 0.0)
        sel = jnp.where(bidx == n, keep, sel)
    return sel


def _attn_prompt_kernel(q_ref, k_ref, v_ref, g_ref, slope_ref, o_ref,
                        kb_ref, vt_ref, s_ref, p_ref, *, n_blocks, scale):
    blk = MOBA_BLOCK
    k = k_ref[0]
    kb_ref[...] = k.astype(BF16)
    vt_ref[...] = v_ref[0].T.astype(BF16)
    kmean = jnp.mean(k.reshape(n_blocks, blk, HEAD_DIM), axis=1)
    q = q_ref[0]
    gate = lax.dot_general(kmean, q, _NT, precision=lax.Precision.HIGHEST,
                           preferred_element_type=F32)
    bidx = lax.broadcasted_iota(jnp.int32, gate.shape, 0)
    qblk = lax.broadcasted_iota(jnp.int32, gate.shape, 1) // blk
    gate = jnp.where(bidx < qblk, gate, NEG_INF)
    sel = _topk_select(gate, n_blocks)

    qs = (q * (scale * LOG2_E)).astype(BF16)
    slope = slope_ref[0] * LOG2_E
    base = (lax.broadcasted_iota(jnp.int32, (blk, blk), 1)
            - lax.broadcasted_iota(jnp.int32, (blk, blk), 0))
    bias = slope * base.astype(F32)
    bias_own = jnp.where(base >= 0, bias, jnp.inf)

    for qi in range(n_blocks):
        cols = slice(qi * blk, (qi + 1) * blk)
        n_keys = (qi + 1) * blk
        slot = qi % 2
        raw = lax.dot_general(kb_ref[0:n_keys, :], qs[cols, :], _NT,
                              preferred_element_type=F32)
        m = None
        for n in range(qi + 1):
            rows = slice(n * blk, (n + 1) * blk)
            if n == qi:
                s = raw[rows, :] - bias_own
            else:
                row_bias = jnp.where(sel[n:n + 1, cols] > 0.0, slope * float((qi - n) * blk), jnp.inf)
                s = (raw[rows, :] - bias) - row_bias
            s_ref[slot, rows, :] = s
            cm = jnp.max(s, axis=0, keepdims=True)
            m = cm if m is None else jnp.maximum(m, cm)
        l = jnp.zeros_like(m)
        for n in range(qi + 1):
            rows = slice(n * blk, (n + 1) * blk)
            p = jnp.exp2(s_ref[slot, rows, :] - m)
            l = l + jnp.sum(p, axis=0, keepdims=True)
            p_ref[slot, rows, :] = p.astype(BF16)
        o_t = jnp.dot(vt_ref[:, 0:n_keys], p_ref[slot, 0:n_keys, :],
                      preferred_element_type=F32)
        o = (o_t / l).T
        ms = jnp.mean(o * o, axis=-1, keepdims=True)
        o_ref[0, cols, :] = (o * lax.rsqrt(ms + EPS) * g_ref[...]).astype(o_ref.dtype)


def _alibi_slopes(n_heads):
    return 2.0 ** (-ALIBI_MAX_BIAS * jnp.arange(1, n_heads + 1, dtype=F32) / n_heads)


def _attn_prompt(q, k, v, g_ao_l):
    nb, t, width = q.shape
    n_heads = width // HEAD_DIM
    n_blocks = t // MOBA_BLOCK
    slopes = jnp.broadcast_to(_alibi_slopes(n_heads)[:, None, None], (n_heads, 1, MOBA_BLOCK))
    kern = functools.partial(_attn_prompt_kernel, n_blocks=n_blocks, scale=HEAD_DIM ** -0.5)
    head_spec = pl.BlockSpec((1, t, HEAD_DIM), lambda b, h: (b, 0, h))
    return pl.pallas_call(
        kern,
        grid=(nb, n_heads),
        in_specs=[
            head_spec, head_spec, head_spec,
            pl.BlockSpec((1, HEAD_DIM), lambda b, h: (0, h)),
            pl.BlockSpec((1, 1, MOBA_BLOCK), lambda b, h: (h, 0, 0)),
        ],
        out_specs=head_spec,
        out_shape=jax.ShapeDtypeStruct((nb, t, width), BF16),
        scratch_shapes=[
            pltpu.VMEM((t, HEAD_DIM), BF16),
            pltpu.VMEM((HEAD_DIM, t), BF16),
            pltpu.VMEM((2, t, MOBA_BLOCK), F32),
            pltpu.VMEM((2, t, MOBA_BLOCK), BF16),
        ],
        compiler_params=_params("arbitrary", "arbitrary"),
        name="attn_prompt",
    )(q, k, v, g_ao_l.reshape(1, width), slopes)


def _topk_select_lanes(gate):
    lane = lax.broadcasted_iota(jnp.int32, gate.shape, 1).astype(F32)
    sel = jnp.zeros(gate.shape, F32)
    for _ in range(MOBA_TOPK):
        mx = jnp.max(gate, axis=1, keepdims=True)
        is_max = (gate == mx) & (mx > NEG_INF)
        first = jnp.min(jnp.where(is_max, lane, float(LANES)), axis=1, keepdims=True)
        pick = lane == first
        sel = jnp.where(pick, 1.0, sel)
        gate = jnp.where(pick, NEG_INF, gate)
    return sel


def _attn_sample_kernel(pt_ref, q_ref, kn_ref, vn_ref, k0_ref, k1_ref, v0_ref, v1_ref,
                        g_ref, slope_ref, o_ref,
                        dq_ref, mb_ref, mall_ref, lall_ref, gall_ref, op_ref,
                        *, n_heads, tq, n_blocks, past_len, scale):
    del pt_ref
    n = pl.program_id(1)
    rows = n_heads * tq
    page_keys = PAGE_SIZE * n_heads
    slope = slope_ref[...]
    q = q_ref[0]
    qb = (q * scale).astype(BF16)
    lane = lax.broadcasted_iota(jnp.int32, (rows, LANES), 1)

    @pl.when(n == 0)
    def _():
        r = lax.broadcasted_iota(jnp.int32, (rows, page_keys), 0)
        c = lax.broadcasted_iota(jnp.int32, (rows, page_keys), 1)
        dq_ref[...] = (r % tq + past_len - c // n_heads).astype(F32)
        mb_ref[...] = jnp.where(c % n_heads == r // tq, 0.0, jnp.inf)
        mall_ref[...] = jnp.full((rows, LANES), NEG_INF, F32)
        lall_ref[...] = jnp.zeros((rows, LANES), F32)
        gall_ref[...] = jnp.full((rows, LANES), NEG_INF, F32)

    def scores(keys, n_keys, page_pos):
        raw = lax.dot_general(qb, keys.astype(BF16), _NT, preferred_element_type=F32)
        dist = dq_ref[:, 0:n_keys] - page_pos
        return (raw - slope * dist) - mb_ref[:, 0:n_keys], dist

    k0 = k0_ref[...]
    k1 = k1_ref[...]
    pos0 = (n * MOBA_BLOCK).astype(F32)
    s0, _ = scores(k0.reshape(page_keys, HEAD_DIM), page_keys, pos0)
    s1, _ = scores(k1.reshape(page_keys, HEAD_DIM), page_keys, pos0 + float(PAGE_SIZE))
    m = jnp.maximum(jnp.max(s0, axis=1, keepdims=True), jnp.max(s1, axis=1, keepdims=True))
    p0 = jnp.exp(s0 - m)
    p1 = jnp.exp(s1 - m)
    l = jnp.sum(p0, axis=1, keepdims=True) + jnp.sum(p1, axis=1, keepdims=True)
    op_ref[n] = (
        jnp.dot(p0.astype(BF16), v0_ref[...].reshape(page_keys, HEAD_DIM).astype(BF16),
                preferred_element_type=F32)
        + jnp.dot(p1.astype(BF16), v1_ref[...].reshape(page_keys, HEAD_DIM).astype(BF16),
                  preferred_element_type=F32))
    kmean = (jnp.sum(k0, axis=0) + jnp.sum(k1, axis=0)) * (1.0 / MOBA_BLOCK)
    kmean_rows = jnp.concatenate(
        [jnp.broadcast_to(kmean[h:h + 1, :], (tq, HEAD_DIM)) for h in range(n_heads)], axis=0)
    g = jnp.sum(q * kmean_rows, axis=1, keepdims=True)
    mall_ref[...] = jnp.where(lane == n, m, mall_ref[...])
    lall_ref[...] = jnp.where(lane == n, l, lall_ref[...])
    gall_ref[...] = jnp.where(lane == n, g, gall_ref[...])

    @pl.when(n == n_blocks - 1)
    def _():
        sel = _topk_select_lanes(gall_ref[...]) > 0.0
        m_all = jnp.where(sel, mall_ref[...], NEG_INF)
        own_keys = tq * n_heads
        s_own, dist = scores(kn_ref[0], own_keys, float(past_len))
        s_own = jnp.where(dist >= 0.0, s_own, NEG_INF)
        m_tot = jnp.maximum(jnp.max(m_all, axis=1, keepdims=True),
                            jnp.max(s_own, axis=1, keepdims=True))
        w = jnp.exp(m_all - m_tot)
        p_own = jnp.exp(s_own - m_tot)
        l_tot = (jnp.sum(w * lall_ref[...], axis=1, keepdims=True)
                 + jnp.sum(p_own, axis=1, keepdims=True))
        acc = jnp.dot(p_own.astype(BF16), vn_ref[0].astype(BF16), preferred_element_type=F32)
        for i in range(n_blocks):
            acc = acc + w[:, i:i + 1] * op_ref[i]
        o = acc / l_tot
        ms = jnp.mean(o * o, axis=-1, keepdims=True)
        on = o * lax.rsqrt(ms + EPS)
        out = jnp.concatenate(
            [on[h * tq:(h + 1) * tq, :] * g_ref[:, h * HEAD_DIM:(h + 1) * HEAD_DIM]
             for h in range(n_heads)], axis=1)
        o_ref[0] = out.astype(o_ref.dtype)


def _attn_sample(q, k_new, v_new, cache_k, cache_v, layer, page_table, g_ao_l):
    nb, tq, width = q.shape
    n_heads = width // HEAD_DIM
    rows = n_heads * tq
    n_pages = page_table.shape[1]
    past_len = n_pages * PAGE_SIZE
    n_blocks = n_pages // PAGES_PER_BLOCK
    assert rows % SUBLANES == 0 and n_blocks <= LANES
    assert past_len % MOBA_BLOCK == 0 and (past_len + tq - 1) // MOBA_BLOCK == n_blocks
    q_rows = q.reshape(nb, tq, n_heads, HEAD_DIM).transpose(0, 2, 1, 3).reshape(nb, rows, HEAD_DIM)
    kn = k_new.reshape(nb, tq * n_heads, HEAD_DIM)
    vn = v_new.reshape(nb, tq * n_heads, HEAD_DIM)
    slope_col = jnp.repeat(_alibi_slopes(n_heads), tq).reshape(rows, 1)
    pt_flat = page_table.reshape(-1)

    def page_spec(which):
        return pl.BlockSpec(
            (None, None, PAGE_SIZE, n_heads, HEAD_DIM),
            lambda b, n, pt: (layer, pt[b * n_pages + PAGES_PER_BLOCK * n + which], 0, 0, 0))

    seq_spec = pl.BlockSpec((1, rows, HEAD_DIM), lambda b, n, pt: (b, 0, 0))
    page_keys = PAGE_SIZE * n_heads
    grid_spec = pltpu.PrefetchScalarGridSpec(
        num_scalar_prefetch=1,
        grid=(nb, n_blocks),
        in_specs=[
            seq_spec,
            pl.BlockSpec((1, tq * n_heads, HEAD_DIM), lambda b, n, pt: (b, 0, 0)),
            pl.BlockSpec((1, tq * n_heads, HEAD_DIM), lambda b, n, pt: (b, 0, 0)),
            page_spec(0), page_spec(1), page_spec(0), page_spec(1),
            pl.BlockSpec((1, width), lambda b, n, pt: (0, 0)),
            pl.BlockSpec((rows, 1), lambda b, n, pt: (0, 0)),
        ],
        out_specs=pl.BlockSpec((1, tq, width), lambda b, n, pt: (b, 0, 0)),
        scratch_shapes=[
            pltpu.VMEM((rows, page_keys), F32),
            pltpu.VMEM((rows, page_keys), F32),
            pltpu.VMEM((rows, LANES), F32),
            pltpu.VMEM((rows, LANES), F32),
            pltpu.VMEM((rows, LANES), F32),
            pltpu.VMEM((n_blocks, rows, HEAD_DIM), F32),
        ],
    )
    kern = functools.partial(_attn_sample_kernel, n_heads=n_heads, tq=tq, n_blocks=n_blocks,
                             past_len=past_len, scale=HEAD_DIM ** -0.5)
    return pl.pallas_call(
        kern,
        grid_spec=grid_spec,
        out_shape=jax.ShapeDtypeStruct((nb, tq, width), BF16),
        compiler_params=_params("arbitrary", "arbitrary"),
        name="attn_sample",
    )(pt_flat, q_rows, kn, vn, cache_k, cache_k, cache_v, cache_v, g_ao_l.reshape(1, width),
      slope_col)


def _layer(x, mod, layer, prev, attend, g1, g2, w_in, conv_w, g_ao, g_co, w_out, w_up, w_down):
    nb, t, d = x.shape
    m = nb * t
    aw = g_ao.shape[1]
    cd = g_co.shape[1]
    h = _norm_mod(x, g1[layer], mod, shift_chunk=0, scale_chunk=1).reshape(m, d)
    q = _mm(h, w_in, layer, 0, aw, F32).reshape(nb, t, aw)
    k = _mm(h, w_in, layer, aw, aw, F32).reshape(nb, t, aw)
    v = _mm(h, w_in, layer, 2 * aw, aw, F32).reshape(nb, t, aw)
    cv = _mm(h, w_in, layer, 3 * aw, 3 * cd, F32).reshape(nb, t, 3 * cd)
    attn = attend(q, k, v, g_ao[layer])
    conv, u_new = _short_conv(cv, prev, conv_w[layer], g_co[layer])
    x = _out_proj(attn.reshape(m, aw), conv.reshape(m, cd), w_out, layer, x, mod, gate_chunk=2)
    h2 = _norm_mod(x, g2[layer], mod, shift_chunk=3, scale_chunk=4).reshape(m, d)
    hid = _mm(h2, w_up, layer, 0, w_up.shape[2], BF16, relu2=True)
    x = _down_proj(hid, w_down, layer, x, mod, gate_chunk=5)
    return x, k, v, u_new


def kernel(x_prompt, x_sample, cache_k, cache_v, state_conv, page_table, c_prompt, c_sample,
           w_ada, b_ada, g_norm1, g_norm2, w_in, conv_w, g_attn_out, g_conv_out,
           w_out, w_up, w_down, g_final):
    depth = w_ada.shape[0]
    bp, tp, d = x_prompt.shape
    bs, ts, _ = x_sample.shape
    aw = g_attn_out.shape[1]
    cd = g_conv_out.shape[1]
    n_heads = aw // HEAD_DIM
    kw = conv_w.shape[1]

    n_c = bp + bs
    c_rows = -(-n_c // SUBLANES) * SUBLANES
    c_all = jnp.concatenate([c_prompt, c_sample, jnp.zeros((c_rows - n_c, d), F32)], axis=0)
    mod = _ada(c_all, w_ada, b_ada)

    u_zero = jnp.zeros((bp, kw - 1, cd), F32)

    weights = (g_norm1, g_norm2, w_in, conv_w, g_attn_out, g_conv_out, w_out, w_up, w_down)
    xp, xs = x_prompt, x_sample
    kp_l, vp_l, cp_l, ks_l, vs_l, cs_l = [], [], [], [], [], []
    for layer in range(depth):
        mod_p = mod[layer, :bp].reshape(bp, 1, -1)
        mod_s = mod[layer, bp:n_c].reshape(bs, 1, -1)
        xp, kp, vp, cp = _layer(xp, mod_p, layer, u_zero, _attn_prompt, *weights)
        attend_s = functools.partial(_attn_sample_wrapper, cache_k=cache_k, cache_v=cache_v,
                                     layer=layer, page_table=page_table)
        xs, kn, vn, cn = _layer(xs, mod_s, layer, state_conv[layer], attend_s, *weights)
        kp_l.append(kp.reshape(bp, tp, n_heads, HEAD_DIM))
        vp_l.append(vp.reshape(bp, tp, n_heads, HEAD_DIM))
        cp_l.append(cp)
        ks_l.append(kn.reshape(bs, ts, n_heads, HEAD_DIM))
        vs_l.append(vn.reshape(bs, ts, n_heads, HEAD_DIM))
        cs_l.append(cn)
    y_prompt = _final_norm(xp, g_final)
    y_sample = _final_norm(xs, g_final)
    return (y_prompt, y_sample, jnp.stack(kp_l), jnp.stack(vp_l), jnp.stack(cp_l),
            jnp.stack(ks_l), jnp.stack(vs_l), jnp.stack(cs_l))


def _attn_sample_wrapper(q, k, v, g_ao_l, *, cache_k, cache_v, layer, page_table):
    return _attn_sample(q, k, v, cache_k, cache_v, layer, page_table, g_ao_l)
```

```python
import functools

import jax
import jax.numpy as jnp
from jax import lax
from jax.experimental import pallas as pl
from jax.experimental.pallas import tpu as pltpu

F32 = jnp.float32
BF16 = jnp.bfloat16

HEAD_DIM = 128
PAGE_SIZE = 128
MOBA_BLOCK = 256
MOBA_TOPK = 3
ALIBI_MAX_BIAS = 8.0
EPS = 1e-6
LOG2_E = 1.4426950408889634
PAGES_PER_BLOCK = MOBA_BLOCK // PAGE_SIZE

V7X_VMEM_BYTES = 64 * 2**20
VMEM_LIMIT_BYTES = V7X_VMEM_BYTES - 8 * 2**20
LANES = 128
SUBLANES = 8

NEG_INF = float("-inf")
_NT = (((1,), (1,)), ((), ()))


def _params(*sem):
    return pltpu.CompilerParams(dimension_semantics=sem, vmem_limit_bytes=VMEM_LIMIT_BYTES)


def _ada_kernel(c_ref, w_ref, b_ref, o_ref):
    c = c_ref[...]
    s = c * jax.nn.sigmoid(c)
    o_ref[...] = jnp.dot(s.astype(BF16), w_ref[...].astype(BF16),
                         preferred_element_type=F32) + b_ref[...]


def _ada(c_all, w_ada, b_ada, tn=512):
    depth, d, cols = w_ada.shape
    rows = c_all.shape[0]
    return pl.pallas_call(
        _ada_kernel,
        grid=(depth, cols // tn),
        in_specs=[
            pl.BlockSpec((rows, d), lambda l, j: (0, 0)),
            pl.BlockSpec((None, d, tn), lambda l, j: (l, 0, j)),
            pl.BlockSpec((None, 1, tn), lambda l, j: (l, 0, j)),
        ],
        out_specs=pl.BlockSpec((None, rows, tn), lambda l, j: (l, 0, j)),
        out_shape=jax.ShapeDtypeStruct((depth, rows, cols), F32),
        compiler_params=_params("arbitrary", "arbitrary"),
        name="ada_mod",
    )(c_all, w_ada, b_ada.reshape(depth, 1, cols))


def _norm_mod_kernel(x_ref, g_ref, sc_ref, sh_ref, o_ref):
    x = x_ref[...]
    ms = jnp.mean(x * x, axis=-1, keepdims=True)
    xn = x * lax.rsqrt(ms + EPS) * g_ref[...]
    o_ref[...] = (xn * (1.0 + sc_ref[...]) + sh_ref[...]).astype(o_ref.dtype)


def _norm_kernel(x_ref, g_ref, o_ref):
    x = x_ref[...]
    ms = jnp.mean(x * x, axis=-1, keepdims=True)
    o_ref[...] = (x * lax.rsqrt(ms + EPS) * g_ref[...]).astype(o_ref.dtype)


def _row_tiles(nb_total, t_total, max_rows):
    if t_total >= max_rows:
        return 1, max_rows
    assert nb_total * t_total <= max_rows
    return nb_total, t_total


def _norm_mod(x, g, mod, shift_chunk, scale_chunk, max_rows=256):
    nbt, t, d = x.shape
    nb, tt = _row_tiles(nbt, t, max_rows)
    tpb = t // tt
    return pl.pallas_call(
        _norm_mod_kernel,
        grid=(nbt // nb, tpb),
        in_specs=[
            pl.BlockSpec((nb, tt, d), lambda b, i: (b, i, 0)),
            pl.BlockSpec((1, d), lambda b, i: (0, 0)),
            pl.BlockSpec((nb, 1, d), lambda b, i: (b, 0, scale_chunk)),
            pl.BlockSpec((nb, 1, d), lambda b, i: (b, 0, shift_chunk)),
        ],
        out_specs=pl.BlockSpec((nb, tt, d), lambda b, i: (b, i, 0)),
        out_shape=jax.ShapeDtypeStruct(x.shape, BF16),
        compiler_params=_params("arbitrary", "arbitrary"),
        name="norm_mod",
    )(x, g.reshape(1, d), mod, mod)


def _final_norm(x, g, max_rows=256):
    nbt, t, d = x.shape
    nb, tt = _row_tiles(nbt, t, max_rows)
    return pl.pallas_call(
        _norm_kernel,
        grid=(nbt // nb, t // tt),
        in_specs=[
            pl.BlockSpec((nb, tt, d), lambda b, i: (b, i, 0)),
            pl.BlockSpec((1, d), lambda b, i: (0, 0)),
        ],
        out_specs=pl.BlockSpec((nb, tt, d), lambda b, i: (b, i, 0)),
        out_shape=jax.ShapeDtypeStruct(x.shape, x.dtype),
        compiler_params=_params("arbitrary", "arbitrary"),
        name="final_norm",
    )(x, g.reshape(1, d))


MM_ROWS = 1024
MM_COLS = 512


def _sample_col(i, j, n_col_tiles):
    return jnp.where(i == 0, j, n_col_tiles - 1)


def _mm_kernel(x_ref, xs_ref, w_ref, o_ref, os_ref, *, relu2):
    w = w_ref[...].astype(BF16)

    def project(x):
        acc = jnp.dot(x, w, preferred_element_type=F32)
        return jnp.square(jnp.maximum(acc, 0.0)) if relu2 else acc

    o_ref[...] = project(x_ref[...]).astype(o_ref.dtype)

    @pl.when(pl.program_id(0) == 0)
    def _():
        os_ref[...] = project(xs_ref[...]).astype(os_ref.dtype)


def _mm(x2d, xs2d, w, layer, out_dtype, relu2=False):
    m, k = x2d.shape
    ms = xs2d.shape[0]
    ncols = w.shape[2]
    tm, tn = MM_ROWS, MM_COLS
    nj = ncols // tn
    return pl.pallas_call(
        functools.partial(_mm_kernel, relu2=relu2),
        grid=(m // tm, nj),
        in_specs=[
            pl.BlockSpec((tm, k), lambda i, j: (i, 0)),
            pl.BlockSpec((ms, k), lambda i, j: (0, 0)),
            pl.BlockSpec((None, k, tn), lambda i, j: (layer, 0, j)),
        ],
        out_specs=[
            pl.BlockSpec((tm, tn), lambda i, j: (i, j)),
            pl.BlockSpec((ms, tn), lambda i, j: (0, _sample_col(i, j, nj))),
        ],
        out_shape=[
            jax.ShapeDtypeStruct((m, ncols), out_dtype),
            jax.ShapeDtypeStruct((ms, ncols), out_dtype),
        ],
        compiler_params=_params("arbitrary", "arbitrary"),
        name="mm",
    )(x2d, xs2d, w)


def _mm_res2_kernel(a_ref, c_ref, as_ref, cs_ref, wa_ref, wc_ref, x_ref, g_ref, xs_ref, gs_ref,
                    o_ref, os_ref):
    wa = wa_ref[...].astype(BF16)
    wc = wc_ref[...].astype(BF16)

    def gated_residual(a, c, x, g):
        acc = (jnp.dot(a, wa, preferred_element_type=F32)
               + jnp.dot(c, wc, preferred_element_type=F32))
        return x + g * acc.reshape(x.shape)

    o_ref[...] = gated_residual(a_ref[...], c_ref[...], x_ref[...], g_ref[...])

    @pl.when(pl.program_id(0) == 0)
    def _():
        os_ref[...] = gated_residual(as_ref[...], cs_ref[...], xs_ref[...], gs_ref[...])


def _out_proj(a2d, c2d, as2d, cs2d, w_out, layer, x, mod, xs, mod_s, gate_chunk):
    nbt, t, d = x.shape
    nbs, ts, _ = xs.shape
    ka = a2d.shape[1]
    tm, tn = MM_ROWS, MM_COLS
    ms = nbs * ts
    tpb = t // tm
    nj = d // tn
    gate_off = gate_chunk * nj

    def sample_block(shape):
        return pl.BlockSpec(shape, lambda i, j: (0, 0, _sample_col(i, j, nj)))

    return pl.pallas_call(
        _mm_res2_kernel,
        grid=(a2d.shape[0] // tm, nj),
        in_specs=[
            pl.BlockSpec((tm, ka), lambda i, j: (i, 0)),
            pl.BlockSpec((tm, ka), lambda i, j: (i, 0)),
            pl.BlockSpec((ms, ka), lambda i, j: (0, 0)),
            pl.BlockSpec((ms, ka), lambda i, j: (0, 0)),
            pl.BlockSpec((None, ka, tn), lambda i, j: (layer, 0, j)),
            pl.BlockSpec((None, ka, tn), lambda i, j: (layer, 1, j)),
            pl.BlockSpec((1, tm, tn), lambda i, j: (i // tpb, i % tpb, j)),
            pl.BlockSpec((1, 1, tn), lambda i, j: (i // tpb, 0, gate_off + j)),
            sample_block((nbs, ts, tn)),
            pl.BlockSpec((nbs, 1, tn), lambda i, j: (0, 0, gate_off + _sample_col(i, j, nj))),
        ],
        out_specs=[
            pl.BlockSpec((1, tm, tn), lambda i, j: (i // tpb, i % tpb, j)),
            sample_block((nbs, ts, tn)),
        ],
        out_shape=[
            jax.ShapeDtypeStruct(x.shape, x.dtype),
            jax.ShapeDtypeStruct(xs.shape, xs.dtype),
        ],
        compiler_params=_params("arbitrary", "arbitrary"),
        name="out_proj",
    )(a2d, c2d, as2d, cs2d, w_out, w_out, x, mod, xs, mod_s)


DOWN_ROWS = 2048
DOWN_COLS = 1024
DOWN_K = 1024


def _mm_res_k_kernel(h_ref, hs_ref, w_ref, x_ref, g_ref, xs_ref, gs_ref, o_ref, os_ref):
    i = pl.program_id(0)
    k = pl.program_id(2)
    last = pl.num_programs(2) - 1
    w = w_ref[...].astype(BF16)

    @pl.when(k == 0)
    def _():
        o_ref[...] = jnp.zeros_like(o_ref)

    o_ref[...] += jnp.dot(h_ref[...], w, preferred_element_type=F32).reshape(o_ref.shape)

    @pl.when(k == last)
    def _():
        o_ref[...] = x_ref[...] + g_ref[...] * o_ref[...]

    @pl.when((i == 0) & (k == 0))
    def _():
        os_ref[...] = jnp.zeros_like(os_ref)

    @pl.when(i == 0)
    def _():
        os_ref[...] += jnp.dot(hs_ref[...], w, preferred_element_type=F32).reshape(os_ref.shape)

    @pl.when((i == 0) & (k == last))
    def _():
        os_ref[...] = xs_ref[...] + gs_ref[...] * os_ref[...]


def _down_proj(h2d, hs2d, w_down, layer, x, mod, xs, mod_s, gate_chunk):
    nbt, t, d = x.shape
    nbs, ts, _ = xs.shape
    kdim = h2d.shape[1]
    tm, tn, tk = DOWN_ROWS, DOWN_COLS, DOWN_K
    ms = nbs * ts
    tpb = t // tm
    nj = d // tn
    nk = kdim // tk
    gate_off = gate_chunk * nj

    def sample_block(shape):
        return pl.BlockSpec(shape, lambda i, j, k: (0, 0, _sample_col(i, j, nj)))

    return pl.pallas_call(
        _mm_res_k_kernel,
        grid=(h2d.shape[0] // tm, nj, nk),
        in_specs=[
            pl.BlockSpec((tm, tk), lambda i, j, k: (i, k)),
            pl.BlockSpec((ms, tk), lambda i, j, k: (0, _sample_col(i, k, nk))),
            pl.BlockSpec((None, tk, tn), lambda i, j, k: (layer, k, j)),
            pl.BlockSpec((1, tm, tn), lambda i, j, k: (i // tpb, i % tpb, j),
                         pipeline_mode=pl.Buffered(1)),
            pl.BlockSpec((1, 1, tn), lambda i, j, k: (i // tpb, 0, gate_off + j)),
            sample_block((nbs, ts, tn)),
            pl.BlockSpec((nbs, 1, tn), lambda i, j, k: (0, 0, gate_off + _sample_col(i, j, nj))),
        ],
        out_specs=[
            pl.BlockSpec((1, tm, tn), lambda i, j, k: (i // tpb, i % tpb, j)),
            sample_block((nbs, ts, tn)),
        ],
        out_shape=[
            jax.ShapeDtypeStruct(x.shape, x.dtype),
            jax.ShapeDtypeStruct(xs.shape, xs.dtype),
        ],
        compiler_params=_params("arbitrary", "arbitrary", "arbitrary"),
        name="down_proj",
    )(h2d, hs2d, w_down, x, mod, xs, mod_s)


_PAD_ROWS = SUBLANES


def _conv_kernel(gb_ref, gc_ref, u_ref, prev_ref, w_ref, g_ref, o_ref, un_ref, pad_ref, *, t, cw, kw):
    u = gc_ref[0] * u_ref[0]
    pad_ref[pl.ds(_PAD_ROWS - (kw - 1), kw - 1), :] = prev_ref[0]
    pad_ref[pl.ds(_PAD_ROWS, t), :] = u
    w = w_ref[...]
    y = None
    for i in range(kw):
        tap = u if i == kw - 1 else pad_ref[pl.ds(_PAD_ROWS - (kw - 1) + i, t), :]
        term = w[i:i + 1, :] * tap
        y = term if y is None else y + term
    co = gb_ref[0] * y
    outs = []
    for gi in range(cw // HEAD_DIM):
        sl = slice(gi * HEAD_DIM, (gi + 1) * HEAD_DIM)
        c = co[:, sl]
        ms = jnp.mean(c * c, axis=-1, keepdims=True)
        outs.append(c * lax.rsqrt(ms + EPS) * g_ref[:, sl])
    o_ref[0] = jnp.concatenate(outs, axis=-1).astype(o_ref.dtype)
    un_ref[0] = pad_ref[pl.ds(_PAD_ROWS + t - (kw - 1), kw - 1), :]


def _short_conv(proj, col0, prev, conv_w_l, g_co_l, cw):
    nb, t, _ = proj.shape
    kw, c = conv_w_l.shape
    ncb = c // cw
    off = col0 // cw
    return pl.pallas_call(
        functools.partial(_conv_kernel, t=t, cw=cw, kw=kw),
        grid=(nb, ncb),
        in_specs=[
            pl.BlockSpec((1, t, cw), lambda b, j: (b, 0, off + j)),
            pl.BlockSpec((1, t, cw), lambda b, j: (b, 0, off + ncb + j)),
            pl.BlockSpec((1, t, cw), lambda b, j: (b, 0, off + 2 * ncb + j)),
            pl.BlockSpec((1, kw - 1, cw), lambda b, j: (b, 0, j)),
            pl.BlockSpec((kw, cw), lambda b, j: (0, j)),
            pl.BlockSpec((1, cw), lambda b, j: (0, j)),
        ],
        out_specs=[
            pl.BlockSpec((1, t, cw), lambda b, j: (b, 0, j)),
            pl.BlockSpec((1, kw - 1, cw), lambda b, j: (b, 0, j)),
        ],
        out_shape=[
            jax.ShapeDtypeStruct((nb, t, c), BF16),
            jax.ShapeDtypeStruct((nb, kw - 1, c), F32),
        ],
        scratch_shapes=[pltpu.VMEM((t + _PAD_ROWS, cw), F32)],
        compiler_params=_params("arbitrary", "arbitrary"),
        name="short_conv",
    )(proj, proj, proj, prev, conv_w_l, g_co_l.reshape(1, c))


def _topk_select(gate, n_blocks):
    bidx = lax.broadcasted_iota(jnp.int32, gate.shape, 0)
    sel = jnp.zeros(gate.shape, F32)
    for n in range(n_blocks):
        row = gate[n:n + 1, :]
        beats = (gate > row) | ((gate == row) & (bidx < n))
        cnt = jnp.sum(beats.astype(F32), axis=0, keepdims=True)
        keep = jnp.where((cnt < MOBA_TOPK) & (row > NEG_INF), 1.0, 0.0)
        sel = jnp.where(bidx == n, keep, sel)
    return sel


def _attn_prompt_kernel(q_ref, k_ref, v_ref, g_ref, slope_ref, o_ref,
                        kb_ref, vt_ref, s_ref, p_ref, *, n_blocks, scale):
    blk = MOBA_BLOCK
    k = k_ref[0]
    kb_ref[...] = k.astype(BF16)
    vt_ref[...] = v_ref[0].T.astype(BF16)
    kmean = jnp.mean(k.reshape(n_blocks, blk, HEAD_DIM), axis=1)
    q = q_ref[0]
    gate = lax.dot_general(kmean, q, _NT, precision=lax.Precision.HIGHEST,
                           preferred_element_type=F32)
    bidx = lax.broadcasted_iota(jnp.int32, gate.shape, 0)
    qblk = lax.broadcasted_iota(jnp.int32, gate.shape, 1) // blk
    gate = jnp.where(bidx < qblk, gate, NEG_INF)
    sel = _topk_select(gate, n_blocks)

    qs = (q * (scale * LOG2_E)).astype(BF16)
    slope = slope_ref[0] * LOG2_E
    base = (lax.broadcasted_iota(jnp.int32, (blk, blk), 1)
            - lax.broadcasted_iota(jnp.int32, (blk, blk), 0))
    bias = slope * base.astype(F32)
    bias_own = jnp.where(base >= 0, bias, jnp.inf)

    for qi in range(n_blocks):
        cols = slice(qi * blk, (qi + 1) * blk)
        n_keys = (qi + 1) * blk
        slot = qi % 2
        raw = lax.dot_general(kb_ref[0:n_keys, :], qs[cols, :], _NT,
                              preferred_element_type=F32)
        m = None
        for n in range(qi + 1):
            rows = slice(n * blk, (n + 1) * blk)
            if n == qi:
                s = raw[rows, :] - bias_own
            else:
                row_bias = jnp.where(sel[n:n + 1, cols] > 0.0, slope * float((qi - n) * blk), jnp.inf)
                s = (raw[rows, :] - bias) - row_bias
            s_ref[slot, rows, :] = s
            cm = jnp.max(s, axis=0, keepdims=True)
            m = cm if m is None else jnp.maximum(m, cm)
        l = jnp.zeros_like(m)
        for n in range(qi + 1):
            rows = slice(n * blk, (n + 1) * blk)
            p = jnp.exp2(s_ref[slot, rows, :] - m)
            l = l + jnp.sum(p, axis=0, keepdims=True)
            p_ref[slot, rows, :] = p.astype(BF16)
        o_t = jnp.dot(vt_ref[:, 0:n_keys], p_ref[slot, 0:n_keys, :],
                      preferred_element_type=F32)
        o = (o_t / l).T
        ms = jnp.mean(o * o, axis=-1, keepdims=True)
        o_ref[0, cols, :] = (o * lax.rsqrt(ms + EPS) * g_ref[...]).astype(o_ref.dtype)


def _alibi_slopes(n_heads):
    return 2.0 ** (-ALIBI_MAX_BIAS * jnp.arange(1, n_heads + 1, dtype=F32) / n_heads)


def _attn_prompt(proj, g_ao_l):
    nb, t, _ = proj.shape
    width = g_ao_l.shape[0]
    n_heads = width // HEAD_DIM
    n_blocks = t // MOBA_BLOCK
    slopes = jnp.broadcast_to(_alibi_slopes(n_heads)[:, None, None], (n_heads, 1, MOBA_BLOCK))
    kern = functools.partial(_attn_prompt_kernel, n_blocks=n_blocks, scale=HEAD_DIM ** -0.5)

    def head_spec(section):
        return pl.BlockSpec((1, t, HEAD_DIM), lambda b, h: (b, 0, section * n_heads + h))

    return pl.pallas_call(
        kern,
        grid=(nb, n_heads),
        in_specs=[
            head_spec(0), head_spec(1), head_spec(2),
            pl.BlockSpec((1, HEAD_DIM), lambda b, h: (0, h)),
            pl.BlockSpec((1, 1, MOBA_BLOCK), lambda b, h: (h, 0, 0)),
        ],
        out_specs=head_spec(0),
        out_shape=jax.ShapeDtypeStruct((nb, t, width), BF16),
        scratch_shapes=[
            pltpu.VMEM((t, HEAD_DIM), BF16),
            pltpu.VMEM((HEAD_DIM, t), BF16),
            pltpu.VMEM((2, t, MOBA_BLOCK), F32),
            pltpu.VMEM((2, t, MOBA_BLOCK), BF16),
        ],
        compiler_params=_params("arbitrary", "arbitrary"),
        name="attn_prompt",
    )(proj, proj, proj, g_ao_l.reshape(1, width), slopes)


def _topk_select_lanes(gate):
    lane = lax.broadcasted_iota(jnp.int32, gate.shape, 1).astype(F32)
    sel = jnp.zeros(gate.shape, F32)
    for _ in range(MOBA_TOPK):
        mx = jnp.max(gate, axis=1, keepdims=True)
        is_max = (gate == mx) & (mx > NEG_INF)
        first = jnp.min(jnp.where(is_max, lane, float(LANES)), axis=1, keepdims=True)
        pick = lane == first
        sel = jnp.where(pick, 1.0, sel)
        gate = jnp.where(pick, NEG_INF, gate)
    return sel


def _attn_sample_kernel(pt_ref, q_ref, kn_ref, vn_ref, k0_ref, k1_ref, v0_ref, v1_ref,
                        g_ref, slope_ref, o_ref,
                        dq_ref, mb_ref, mall_ref, lall_ref, gall_ref, op_ref,
                        *, n_heads, tq, n_blocks, past_len, scale):
    del pt_ref
    n = pl.program_id(1)
    rows = n_heads * tq
    page_keys = PAGE_SIZE * n_heads
    slope = slope_ref[...]
    q = q_ref[0]
    qb = (q * scale).astype(BF16)
    lane = lax.broadcasted_iota(jnp.int32, (rows, LANES), 1)

    @pl.when(n == 0)
    def _():
        r = lax.broadcasted_iota(jnp.int32, (rows, page_keys), 0)
        c = lax.broadcasted_iota(jnp.int32, (rows, page_keys), 1)
        dq_ref[...] = (r % tq + past_len - c // n_heads).astype(F32)
        mb_ref[...] = jnp.where(c % n_heads == r // tq, 0.0, jnp.inf)
        mall_ref[...] = jnp.full((rows, LANES), NEG_INF, F32)
        lall_ref[...] = jnp.zeros((rows, LANES), F32)
        gall_ref[...] = jnp.full((rows, LANES), NEG_INF, F32)

    def scores(keys, n_keys, page_pos):
        raw = lax.dot_general(qb, keys.astype(BF16), _NT, preferred_element_type=F32)
        dist = dq_ref[:, 0:n_keys] - page_pos
        return (raw - slope * dist) - mb_ref[:, 0:n_keys], dist

    k0 = k0_ref[...]
    k1 = k1_ref[...]
    pos0 = (n * MOBA_BLOCK).astype(F32)
    s0, _ = scores(k0.reshape(page_keys, HEAD_DIM), page_keys, pos0)
    s1, _ = scores(k1.reshape(page_keys, HEAD_DIM), page_keys, pos0 + float(PAGE_SIZE))
    m = jnp.maximum(jnp.max(s0, axis=1, keepdims=True), jnp.max(s1, axis=1, keepdims=True))
    p0 = jnp.exp(s0 - m)
    p1 = jnp.exp(s1 - m)
    l = jnp.sum(p0, axis=1, keepdims=True) + jnp.sum(p1, axis=1, keepdims=True)
    op_ref[n] = (
        jnp.dot(p0.astype(BF16), v0_ref[...].reshape(page_keys, HEAD_DIM).astype(BF16),
                preferred_element_type=F32)
        + jnp.dot(p1.astype(BF16), v1_ref[...].reshape(page_keys, HEAD_DIM).astype(BF16),
                  preferred_element_type=F32))
    kmean = (jnp.sum(k0, axis=0) + jnp.sum(k1, axis=0)) * (1.0 / MOBA_BLOCK)
    kmean_rows = jnp.concatenate(
        [jnp.broadcast_to(kmean[h:h + 1, :], (tq, HEAD_DIM)) for h in range(n_heads)], axis=0)
    g = jnp.sum(q * kmean_rows, axis=1, keepdims=True)
    mall_ref[...] = jnp.where(lane == n, m, mall_ref[...])
    lall_ref[...] = jnp.where(lane == n, l, lall_ref[...])
    gall_ref[...] = jnp.where(lane == n, g, gall_ref[...])

    @pl.when(n == n_blocks - 1)
    def _():
        sel = _topk_select_lanes(gall_ref[...]) > 0.0
        m_all = jnp.where(sel, mall_ref[...], NEG_INF)
        own_keys = tq * n_heads
        s_own, dist = scores(kn_ref[0], own_keys, float(past_len))
        s_own = jnp.where(dist >= 0.0, s_own, NEG_INF)
        m_tot = jnp.maximum(jnp.max(m_all, axis=1, keepdims=True),
                            jnp.max(s_own, axis=1, keepdims=True))
        w = jnp.exp(m_all - m_tot)
        p_own = jnp.exp(s_own - m_tot)
        l_tot = (jnp.sum(w * lall_ref[...], axis=1, keepdims=True)
                 + jnp.sum(p_own, axis=1, keepdims=True))
        acc = jnp.dot(p_own.astype(BF16), vn_ref[0].astype(BF16), preferred_element_type=F32)
        for i in range(n_blocks):
            acc = acc + w[:, i:i + 1] * op_ref[i]
        o = acc / l_tot
        ms = jnp.mean(o * o, axis=-1, keepdims=True)
        on = o * lax.rsqrt(ms + EPS)
        out = jnp.concatenate(
            [on[h * tq:(h + 1) * tq, :] * g_ref[:, h * HEAD_DIM:(h + 1) * HEAD_DIM]
             for h in range(n_heads)], axis=1)
        o_ref[0] = out.astype(o_ref.dtype)


def _attn_sample(q, k_new, v_new, cache_k, cache_v, layer, page_table, g_ao_l):
    nb, tq, width = q.shape
    n_heads = width // HEAD_DIM
    rows = n_heads * tq
    n_pages = page_table.shape[1]
    past_len = n_pages * PAGE_SIZE
    n_blocks = n_pages // PAGES_PER_BLOCK
    assert rows % SUBLANES == 0 and n_blocks <= LANES
    assert past_len % MOBA_BLOCK == 0 and (past_len + tq - 1) // MOBA_BLOCK == n_blocks
    q_rows = q.reshape(nb, tq, n_heads, HEAD_DIM).transpose(0, 2, 1, 3).reshape(nb, rows, HEAD_DIM)
    kn = k_new.reshape(nb, tq * n_heads, HEAD_DIM)
    vn = v_new.reshape(nb, tq * n_heads, HEAD_DIM)
    slope_col = jnp.repeat(_alibi_slopes(n_heads), tq).reshape(rows, 1)
    pt_flat = page_table.reshape(-1)

    def page_spec(which):
        return pl.BlockSpec(
            (None, None, PAGE_SIZE, n_heads, HEAD_DIM),
            lambda b, n, pt: (layer, pt[b * n_pages + PAGES_PER_BLOCK * n + which], 0, 0, 0))

    seq_spec = pl.BlockSpec((1, rows, HEAD_DIM), lambda b, n, pt: (b, 0, 0))
    page_keys = PAGE_SIZE * n_heads
    grid_spec = pltpu.PrefetchScalarGridSpec(
        num_scalar_prefetch=1,
        grid=(nb, n_blocks),
        in_specs=[
            seq_spec,
            pl.BlockSpec((1, tq * n_heads, HEAD_DIM), lambda b, n, pt: (b, 0, 0)),
            pl.BlockSpec((1, tq * n_heads, HEAD_DIM), lambda b, n, pt: (b, 0, 0)),
            page_spec(0), page_spec(1), page_spec(0), page_spec(1),
            pl.BlockSpec((1, width), lambda b, n, pt: (0, 0)),
            pl.BlockSpec((rows, 1), lambda b, n, pt: (0, 0)),
        ],
        out_specs=pl.BlockSpec((1, tq, width), lambda b, n, pt: (b, 0, 0)),
        scratch_shapes=[
            pltpu.VMEM((rows, page_keys), F32),
            pltpu.VMEM((rows, page_keys), F32),
            pltpu.VMEM((rows, LANES), F32),
            pltpu.VMEM((rows, LANES), F32),
            pltpu.VMEM((rows, LANES), F32),
            pltpu.VMEM((n_blocks, rows, HEAD_DIM), F32),
        ],
    )
    kern = functools.partial(_attn_sample_kernel, n_heads=n_heads, tq=tq, n_blocks=n_blocks,
                             past_len=past_len, scale=HEAD_DIM ** -0.5)
    return pl.pallas_call(
        kern,
        grid_spec=grid_spec,
        out_shape=jax.ShapeDtypeStruct((nb, tq, width), BF16),
        compiler_params=_params("arbitrary", "arbitrary"),
        name="attn_sample",
    )(pt_flat, q_rows, kn, vn, cache_k, cache_k, cache_v, cache_v, g_ao_l.reshape(1, width),
      slope_col)


CONV_COLS_PROMPT = 256
CONV_COLS_SAMPLE = 2048


def _layer(xp, xs, mod_p, mod_s, layer, prev_p, prev_s, cache_k, cache_v, page_table,
           g1, g2, w_in, conv_w, g_ao, g_co, w_out, w_up, w_down):
    bp, tp, d = xp.shape
    bs, ts, _ = xs.shape
    mp, ms = bp * tp, bs * ts
    aw = g_ao.shape[1]
    cd = g_co.shape[1]
    hp = _norm_mod(xp, g1[layer], mod_p, shift_chunk=0, scale_chunk=1).reshape(mp, d)
    hs = _norm_mod(xs, g1[layer], mod_s, shift_chunk=0, scale_chunk=1).reshape(ms, d)
    proj_p, proj_s = _mm(hp, hs, w_in, layer, F32)
    proj_p = proj_p.reshape(bp, tp, -1)
    proj_s = proj_s.reshape(bs, ts, -1)
    attn_p = _attn_prompt(proj_p, g_ao[layer])
    attn_s = _attn_sample(proj_s[..., :aw], proj_s[..., aw:2 * aw], proj_s[..., 2 * aw:3 * aw],
                          cache_k, cache_v, layer, page_table, g_ao[layer])
    conv_p, un_p = _short_conv(proj_p, 3 * aw, prev_p, conv_w[layer], g_co[layer], CONV_COLS_PROMPT)
    conv_s, un_s = _short_conv(proj_s, 3 * aw, prev_s, conv_w[layer], g_co[layer], CONV_COLS_SAMPLE)
    xp, xs = _out_proj(attn_p.reshape(mp, aw), conv_p.reshape(mp, cd),
                       attn_s.reshape(ms, aw), conv_s.reshape(ms, cd),
                       w_out, layer, xp, mod_p, xs, mod_s, gate_chunk=2)
    h2p = _norm_mod(xp, g2[layer], mod_p, shift_chunk=3, scale_chunk=4).reshape(mp, d)
    h2s = _norm_mod(xs, g2[layer], mod_s, shift_chunk=3, scale_chunk=4).reshape(ms, d)
    hid_p, hid_s = _mm(h2p, h2s, w_up, layer, BF16, relu2=True)
    xp, xs = _down_proj(hid_p, hid_s, w_down, layer, xp, mod_p, xs, mod_s, gate_chunk=5)
    return xp, xs, proj_p, proj_s, un_p, un_s


def kernel(x_prompt, x_sample, cache_k, cache_v, state_conv, page_table, c_prompt, c_sample,
           w_ada, b_ada, g_norm1, g_norm2, w_in, conv_w, g_attn_out, g_conv_out,
           w_out, w_up, w_down, g_final):
    depth = w_ada.shape[0]
    bp, tp, d = x_prompt.shape
    bs, ts, _ = x_sample.shape
    aw = g_attn_out.shape[1]
    cd = g_conv_out.shape[1]
    n_heads = aw // HEAD_DIM
    kw = conv_w.shape[1]

    n_c = bp + bs
    c_rows = -(-n_c // SUBLANES) * SUBLANES
    c_all = jnp.concatenate([c_prompt, c_sample, jnp.zeros((c_rows - n_c, d), F32)], axis=0)
    mod = _ada(c_all, w_ada, b_ada)

    u_zero = jnp.zeros((bp, kw - 1, cd), F32)

    weights = (g_norm1, g_norm2, w_in, conv_w, g_attn_out, g_conv_out, w_out, w_up, w_down)
    xp, xs = x_prompt, x_sample
    kp_l, vp_l, cp_l, ks_l, vs_l, cs_l = [], [], [], [], [], []
    for layer in range(depth):
        mod_p = mod[layer, :bp].reshape(bp, 1, -1)
        mod_s = mod[layer, bp:n_c].reshape(bs, 1, -1)
        xp, xs, proj_p, proj_s, cp, cn = _layer(
            xp, xs, mod_p, mod_s, layer, u_zero, state_conv[layer], cache_k, cache_v, page_table,
            *weights)
        kp_l.append(proj_p[..., aw:2 * aw].reshape(bp, tp, n_heads, HEAD_DIM))
        vp_l.append(proj_p[..., 2 * aw:3 * aw].reshape(bp, tp, n_heads, HEAD_DIM))
        cp_l.append(cp)
        ks_l.append(proj_s[..., aw:2 * aw].reshape(bs, ts, n_heads, HEAD_DIM))
        vs_l.append(proj_s[..., 2 * aw:3 * aw].reshape(bs, ts, n_heads, HEAD_DIM))
        cs_l.append(cn)
    y_prompt = _final_norm(xp, g_final)
    y_sample = _final_norm(xs, g_final)
    return (y_prompt, y_sample, jnp.stack(kp_l), jnp.stack(vp_l), jnp.stack(cp_l),
            jnp.stack(ks_l), jnp.stack(vs_l), jnp.stack(cs_l))
```

```python
import functools

import jax
import jax.numpy as jnp
from jax import lax
from jax.experimental import pallas as pl
from jax.experimental.pallas import tpu as pltpu

F32 = jnp.float32
BF16 = jnp.bfloat16

HEAD_DIM = 128
PAGE_SIZE = 128
MOBA_BLOCK = 256
MOBA_TOPK = 3
ALIBI_MAX_BIAS = 8.0
EPS = 1e-6
LOG2_E = 1.4426950408889634
PAGES_PER_BLOCK = MOBA_BLOCK // PAGE_SIZE

V7X_VMEM_BYTES = 64 * 2**20
VMEM_LIMIT_BYTES = V7X_VMEM_BYTES - 8 * 2**20
LANES = 128
SUBLANES = 8

NEG_INF = float("-inf")
_NT = (((1,), (1,)), ((), ()))


def _params(*sem):
    return pltpu.CompilerParams(dimension_semantics=sem, vmem_limit_bytes=VMEM_LIMIT_BYTES)


def _ada_kernel(c_ref, w_ref, b_ref, o_ref):
    c = c_ref[...]
    s = c * jax.nn.sigmoid(c)
    o_ref[...] = jnp.dot(s.astype(BF16), w_ref[...].astype(BF16),
                         preferred_element_type=F32) + b_ref[...]


def _ada(c_all, w_ada, b_ada, tn=512):
    depth, d, cols = w_ada.shape
    rows = c_all.shape[0]
    return pl.pallas_call(
        _ada_kernel,
        grid=(depth, cols // tn),
        in_specs=[
            pl.BlockSpec((rows, d), lambda l, j: (0, 0)),
            pl.BlockSpec((None, d, tn), lambda l, j: (l, 0, j)),
            pl.BlockSpec((None, 1, tn), lambda l, j: (l, 0, j)),
        ],
        out_specs=pl.BlockSpec((None, rows, tn), lambda l, j: (l, 0, j)),
        out_shape=jax.ShapeDtypeStruct((depth, rows, cols), F32),
        compiler_params=_params("arbitrary", "arbitrary"),
        name="ada_mod",
    )(c_all, w_ada, b_ada.reshape(depth, 1, cols))


def _norm_mod_kernel(x_ref, g_ref, sc_ref, sh_ref, o_ref):
    x = x_ref[...]
    ms = jnp.mean(x * x, axis=-1, keepdims=True)
    xn = x * lax.rsqrt(ms + EPS) * g_ref[...]
    o_ref[...] = (xn * (1.0 + sc_ref[...]) + sh_ref[...]).astype(o_ref.dtype)


def _norm_kernel(x_ref, g_ref, o_ref):
    x = x_ref[...]
    ms = jnp.mean(x * x, axis=-1, keepdims=True)
    o_ref[...] = (x * lax.rsqrt(ms + EPS) * g_ref[...]).astype(o_ref.dtype)


def _row_tiles(nb_total, t_total, max_rows):
    if t_total >= max_rows:
        return 1, max_rows
    assert nb_total * t_total <= max_rows
    return nb_total, t_total


def _norm_mod(x, g, mod, shift_chunk, scale_chunk, max_rows=256):
    nbt, t, d = x.shape
    nb, tt = _row_tiles(nbt, t, max_rows)
    tpb = t // tt
    return pl.pallas_call(
        _norm_mod_kernel,
        grid=(nbt // nb, tpb),
        in_specs=[
            pl.BlockSpec((nb, tt, d), lambda b, i: (b, i, 0)),
            pl.BlockSpec((1, d), lambda b, i: (0, 0)),
            pl.BlockSpec((nb, 1, d), lambda b, i: (b, 0, scale_chunk)),
            pl.BlockSpec((nb, 1, d), lambda b, i: (b, 0, shift_chunk)),
        ],
        out_specs=pl.BlockSpec((nb, tt, d), lambda b, i: (b, i, 0)),
        out_shape=jax.ShapeDtypeStruct(x.shape, BF16),
        compiler_params=_params("arbitrary", "arbitrary"),
        name="norm_mod",
    )(x, g.reshape(1, d), mod, mod)


def _final_norm(x, g, max_rows=256):
    nbt, t, d = x.shape
    nb, tt = _row_tiles(nbt, t, max_rows)
    return pl.pallas_call(
        _norm_kernel,
        grid=(nbt // nb, t // tt),
        in_specs=[
            pl.BlockSpec((nb, tt, d), lambda b, i: (b, i, 0)),
            pl.BlockSpec((1, d), lambda b, i: (0, 0)),
        ],
        out_specs=pl.BlockSpec((nb, tt, d), lambda b, i: (b, i, 0)),
        out_shape=jax.ShapeDtypeStruct(x.shape, x.dtype),
        compiler_params=_params("arbitrary", "arbitrary"),
        name="final_norm",
    )(x, g.reshape(1, d))


MM_ROWS = 1024
MM_COLS = 512


def _sample_col(i, j, n_col_tiles):
    return jnp.where(i == 0, j, n_col_tiles - 1)


def _mm_kernel(x_ref, xs_ref, w_ref, o_ref, os_ref, *, relu2):
    w = w_ref[...].astype(BF16)

    def project(x):
        acc = jnp.dot(x, w, preferred_element_type=F32)
        return jnp.square(jnp.maximum(acc, 0.0)) if relu2 else acc

    o_ref[...] = project(x_ref[...]).astype(o_ref.dtype)

    @pl.when(pl.program_id(0) == 0)
    def _():
        os_ref[...] = project(xs_ref[...]).astype(os_ref.dtype)


def _mm(x2d, xs2d, w, layer, col0, ncols, out_dtype, relu2=False):
    m, k = x2d.shape
    ms = xs2d.shape[0]
    tm, tn = MM_ROWS, MM_COLS
    nj = ncols // tn
    off = col0 // tn
    return pl.pallas_call(
        functools.partial(_mm_kernel, relu2=relu2),
        grid=(m // tm, nj),
        in_specs=[
            pl.BlockSpec((tm, k), lambda i, j: (i, 0)),
            pl.BlockSpec((ms, k), lambda i, j: (0, 0)),
            pl.BlockSpec((None, k, tn), lambda i, j: (layer, 0, off + j)),
        ],
        out_specs=[
            pl.BlockSpec((tm, tn), lambda i, j: (i, j)),
            pl.BlockSpec((ms, tn), lambda i, j: (0, _sample_col(i, j, nj))),
        ],
        out_shape=[
            jax.ShapeDtypeStruct((m, ncols), out_dtype),
            jax.ShapeDtypeStruct((ms, ncols), out_dtype),
        ],
        compiler_params=_params("arbitrary", "arbitrary"),
        name="mm",
    )(x2d, xs2d, w)


def _mm_res2_kernel(a_ref, c_ref, as_ref, cs_ref, wa_ref, wc_ref, x_ref, g_ref, xs_ref, gs_ref,
                    o_ref, os_ref):
    wa = wa_ref[...].astype(BF16)
    wc = wc_ref[...].astype(BF16)

    def gated_residual(a, c, x, g):
        acc = (jnp.dot(a, wa, preferred_element_type=F32)
               + jnp.dot(c, wc, preferred_element_type=F32))
        return x + g * acc.reshape(x.shape)

    o_ref[...] = gated_residual(a_ref[...], c_ref[...], x_ref[...], g_ref[...])

    @pl.when(pl.program_id(0) == 0)
    def _():
        os_ref[...] = gated_residual(as_ref[...], cs_ref[...], xs_ref[...], gs_ref[...])


def _out_proj(a2d, c2d, as2d, cs2d, w_out, layer, x, mod, xs, mod_s, gate_chunk):
    nbt, t, d = x.shape
    nbs, ts, _ = xs.shape
    ka = a2d.shape[1]
    tm, tn = MM_ROWS, MM_COLS
    ms = nbs * ts
    tpb = t // tm
    nj = d // tn
    gate_off = gate_chunk * nj

    def sample_block(shape):
        return pl.BlockSpec(shape, lambda i, j: (0, 0, _sample_col(i, j, nj)))

    return pl.pallas_call(
        _mm_res2_kernel,
        grid=(a2d.shape[0] // tm, nj),
        in_specs=[
            pl.BlockSpec((tm, ka), lambda i, j: (i, 0)),
            pl.BlockSpec((tm, ka), lambda i, j: (i, 0)),
            pl.BlockSpec((ms, ka), lambda i, j: (0, 0)),
            pl.BlockSpec((ms, ka), lambda i, j: (0, 0)),
            pl.BlockSpec((None, ka, tn), lambda i, j: (layer, 0, j)),
            pl.BlockSpec((None, ka, tn), lambda i, j: (layer, 1, j)),
            pl.BlockSpec((1, tm, tn), lambda i, j: (i // tpb, i % tpb, j)),
            pl.BlockSpec((1, 1, tn), lambda i, j: (i // tpb, 0, gate_off + j)),
            sample_block((nbs, ts, tn)),
            pl.BlockSpec((nbs, 1, tn), lambda i, j: (0, 0, gate_off + _sample_col(i, j, nj))),
        ],
        out_specs=[
            pl.BlockSpec((1, tm, tn), lambda i, j: (i // tpb, i % tpb, j)),
            sample_block((nbs, ts, tn)),
        ],
        out_shape=[
            jax.ShapeDtypeStruct(x.shape, x.dtype),
            jax.ShapeDtypeStruct(xs.shape, xs.dtype),
        ],
        compiler_params=_params("arbitrary", "arbitrary"),
        name="out_proj",
    )(a2d, c2d, as2d, cs2d, w_out, w_out, x, mod, xs, mod_s)


DOWN_ROWS = 2048
DOWN_COLS = 1024
DOWN_K = 1024
DOWN_COL_CHUNK = 512


def _mm_res_k_kernel(h_ref, hs_ref, w_ref, x_ref, g_ref, xs_ref, gs_ref, o_ref, os_ref):
    i = pl.program_id(0)
    k = pl.program_id(2)
    last = pl.num_programs(2) - 1
    col_chunks = [slice(c, c + DOWN_COL_CHUNK) for c in range(0, w_ref.shape[1], DOWN_COL_CHUNK)]

    def accumulate(acc_ref, h):
        rows = acc_ref.shape[:-1]
        for cols in col_chunks:
            part = jnp.dot(h, w_ref[:, cols].astype(BF16), preferred_element_type=F32)
            acc_ref[:, :, cols] += part.reshape(*rows, DOWN_COL_CHUNK)

    @pl.when(k == 0)
    def _():
        o_ref[...] = jnp.zeros_like(o_ref)

    accumulate(o_ref, h_ref[...])

    @pl.when(k == last)
    def _():
        o_ref[...] = x_ref[...] + g_ref[...] * o_ref[...]

    @pl.when((i == 0) & (k == 0))
    def _():
        os_ref[...] = jnp.zeros_like(os_ref)

    @pl.when(i == 0)
    def _():
        accumulate(os_ref, hs_ref[...])

    @pl.when((i == 0) & (k == last))
    def _():
        os_ref[...] = xs_ref[...] + gs_ref[...] * os_ref[...]


def _down_proj(h2d, hs2d, w_down, layer, x, mod, xs, mod_s, gate_chunk):
    nbt, t, d = x.shape
    nbs, ts, _ = xs.shape
    kdim = h2d.shape[1]
    tm, tn, tk = DOWN_ROWS, DOWN_COLS, DOWN_K
    ms = nbs * ts
    tpb = t // tm
    nj = d // tn
    nk = kdim // tk
    gate_off = gate_chunk * nj

    def sample_block(shape):
        return pl.BlockSpec(shape, lambda i, j, k: (0, 0, _sample_col(i, j, nj)))

    return pl.pallas_call(
        _mm_res_k_kernel,
        grid=(h2d.shape[0] // tm, nj, nk),
        in_specs=[
            pl.BlockSpec((tm, tk), lambda i, j, k: (i, k)),
            pl.BlockSpec((ms, tk), lambda i, j, k: (0, _sample_col(i, k, nk))),
            pl.BlockSpec((None, tk, tn), lambda i, j, k: (layer, k, j)),
            pl.BlockSpec((1, tm, tn), lambda i, j, k: (i // tpb, i % tpb, j),
                         pipeline_mode=pl.Buffered(1)),
            pl.BlockSpec((1, 1, tn), lambda i, j, k: (i // tpb, 0, gate_off + j)),
            sample_block((nbs, ts, tn)),
            pl.BlockSpec((nbs, 1, tn), lambda i, j, k: (0, 0, gate_off + _sample_col(i, j, nj))),
        ],
        out_specs=[
            pl.BlockSpec((1, tm, tn), lambda i, j, k: (i // tpb, i % tpb, j)),
            sample_block((nbs, ts, tn)),
        ],
        out_shape=[
            jax.ShapeDtypeStruct(x.shape, x.dtype),
            jax.ShapeDtypeStruct(xs.shape, xs.dtype),
        ],
        compiler_params=_params("arbitrary", "arbitrary", "arbitrary"),
        name="down_proj",
    )(h2d, hs2d, w_down, x, mod, xs, mod_s)


_PAD_ROWS = SUBLANES


def _conv_kernel(gb_ref, gc_ref, u_ref, prev_ref, w_ref, g_ref, o_ref, un_ref, pad_ref, *, t, cw, kw):
    u = gc_ref[0] * u_ref[0]
    pad_ref[pl.ds(_PAD_ROWS - (kw - 1), kw - 1), :] = prev_ref[0]
    pad_ref[pl.ds(_PAD_ROWS, t), :] = u
    w = w_ref[...]
    y = None
    for i in range(kw):
        tap = u if i == kw - 1 else pad_ref[pl.ds(_PAD_ROWS - (kw - 1) + i, t), :]
        term = w[i:i + 1, :] * tap
        y = term if y is None else y + term
    co = gb_ref[0] * y
    outs = []
    for gi in range(cw // HEAD_DIM):
        sl = slice(gi * HEAD_DIM, (gi + 1) * HEAD_DIM)
        c = co[:, sl]
        ms = jnp.mean(c * c, axis=-1, keepdims=True)
        outs.append(c * lax.rsqrt(ms + EPS) * g_ref[:, sl])
    o_ref[0] = jnp.concatenate(outs, axis=-1).astype(o_ref.dtype)
    un_ref[0] = pad_ref[pl.ds(_PAD_ROWS + t - (kw - 1), kw - 1), :]


def _short_conv(cv, prev, conv_w_l, g_co_l, cw):
    nb, t, _ = cv.shape
    kw, c = conv_w_l.shape
    ncb = c // cw
    return pl.pallas_call(
        functools.partial(_conv_kernel, t=t, cw=cw, kw=kw),
        grid=(nb, ncb),
        in_specs=[
            pl.BlockSpec((1, t, cw), lambda b, j: (b, 0, j)),
            pl.BlockSpec((1, t, cw), lambda b, j: (b, 0, ncb + j)),
            pl.BlockSpec((1, t, cw), lambda b, j: (b, 0, 2 * ncb + j)),
            pl.BlockSpec((1, kw - 1, cw), lambda b, j: (b, 0, j)),
            pl.BlockSpec((kw, cw), lambda b, j: (0, j)),
            pl.BlockSpec((1, cw), lambda b, j: (0, j)),
        ],
        out_specs=[
            pl.BlockSpec((1, t, cw), lambda b, j: (b, 0, j)),
            pl.BlockSpec((1, kw - 1, cw), lambda b, j: (b, 0, j)),
        ],
        out_shape=[
            jax.ShapeDtypeStruct((nb, t, c), BF16),
            jax.ShapeDtypeStruct((nb, kw - 1, c), F32),
        ],
        scratch_shapes=[pltpu.VMEM((t + _PAD_ROWS, cw), F32)],
        compiler_params=_params("arbitrary", "arbitrary"),
        name="short_conv",
    )(cv, cv, cv, prev, conv_w_l, g_co_l.reshape(1, c))


def _topk_select(gate, n_blocks):
    bidx = lax.broadcasted_iota(jnp.int32, gate.shape, 0)
    sel = jnp.zeros(gate.shape, F32)
    for n in range(n_blocks):
        row = gate[n:n + 1, :]
        beats = (gate > row) | ((gate == row) & (bidx < n))
        cnt = jnp.sum(beats.astype(F32), axis=0, keepdims=True)
        keep = jnp.where((cnt < MOBA_TOPK) & (row > NEG_INF), 1.0, 0.0)
        sel = jnp.where(bidx == n, keep, sel)
    return sel


def _attn_prompt_kernel(q_ref, k_ref, v_ref, g_ref, slope_ref, o_ref,
                        kb_ref, vt_ref, s_ref, p_ref, *, n_blocks, scale):
    blk = MOBA_BLOCK
    k = k_ref[0]
    kb_ref[...] = k.astype(BF16)
    vt_ref[...] = v_ref[0].T.astype(BF16)
    kmean = jnp.mean(k.reshape(n_blocks, blk, HEAD_DIM), axis=1)
    q = q_ref[0]
    gate = lax.dot_general(kmean, q, _NT, precision=lax.Precision.HIGHEST,
                           preferred_element_type=F32)
    bidx = lax.broadcasted_iota(jnp.int32, gate.shape, 0)
    qblk = lax.broadcasted_iota(jnp.int32, gate.shape, 1) // blk
    gate = jnp.where(bidx < qblk, gate, NEG_INF)
    sel = _topk_select(gate, n_blocks)

    qs = (q * (scale * LOG2_E)).astype(BF16)
    slope = slope_ref[0] * LOG2_E
    base = (lax.broadcasted_iota(jnp.int32, (blk, blk), 1)
            - lax.broadcasted_iota(jnp.int32, (blk, blk), 0))
    bias = slope * base.astype(F32)
    bias_own = jnp.where(base >= 0, bias, jnp.inf)

    for qi in range(n_blocks):
        cols = slice(qi * blk, (qi + 1) * blk)
        n_keys = (qi + 1) * blk
        slot = qi % 2
        raw = lax.dot_general(kb_ref[0:n_keys, :], qs[cols, :], _NT,
                              preferred_element_type=F32)
        m = None
        for n in range(qi + 1):
            rows = slice(n * blk, (n + 1) * blk)
            if n == qi:
                s = raw[rows, :] - bias_own
            else:
                row_bias = jnp.where(sel[n:n + 1, cols] > 0.0, slope * float((qi - n) * blk), jnp.inf)
                s = (raw[rows, :] - bias) - row_bias
            s_ref[slot, rows, :] = s
            cm = jnp.max(s, axis=0, keepdims=True)
            m = cm if m is None else jnp.maximum(m, cm)
        l = jnp.zeros_like(m)
        for n in range(qi + 1):
            rows = slice(n * blk, (n + 1) * blk)
            p = jnp.exp2(s_ref[slot, rows, :] - m)
            l = l + jnp.sum(p, axis=0, keepdims=True)
            p_ref[slot, rows, :] = p.astype(BF16)
        o_t = jnp.dot(vt_ref[:, 0:n_keys], p_ref[slot, 0:n_keys, :],
                      preferred_element_type=F32)
        o = (o_t / l).T
        ms = jnp.mean(o * o, axis=-1, keepdims=True)
        o_ref[0, cols, :] = (o * lax.rsqrt(ms + EPS) * g_ref[...]).astype(o_ref.dtype)


def _alibi_slopes(n_heads):
    return 2.0 ** (-ALIBI_MAX_BIAS * jnp.arange(1, n_heads + 1, dtype=F32) / n_heads)


def _attn_prompt(q, k, v, g_ao_l):
    nb, t, width = q.shape
    n_heads = width // HEAD_DIM
    n_blocks = t // MOBA_BLOCK
    slopes = jnp.broadcast_to(_alibi_slopes(n_heads)[:, None, None], (n_heads, 1, MOBA_BLOCK))
    kern = functools.partial(_attn_prompt_kernel, n_blocks=n_blocks, scale=HEAD_DIM ** -0.5)
    head_spec = pl.BlockSpec((1, t, HEAD_DIM), lambda b, h: (b, 0, h))
    return pl.pallas_call(
        kern,
        grid=(nb, n_heads),
        in_specs=[
            head_spec, head_spec, head_spec,
            pl.BlockSpec((1, HEAD_DIM), lambda b, h: (0, h)),
            pl.BlockSpec((1, 1, MOBA_BLOCK), lambda b, h: (h, 0, 0)),
        ],
        out_specs=head_spec,
        out_shape=jax.ShapeDtypeStruct((nb, t, width), BF16),
        scratch_shapes=[
            pltpu.VMEM((t, HEAD_DIM), BF16),
            pltpu.VMEM((HEAD_DIM, t), BF16),
            pltpu.VMEM((2, t, MOBA_BLOCK), F32),
            pltpu.VMEM((2, t, MOBA_BLOCK), BF16),
        ],
        compiler_params=_params("arbitrary", "arbitrary"),
        name="attn_prompt",
    )(q, k, v, g_ao_l.reshape(1, width), slopes)


def _topk_select_lanes(gate):
    lane = lax.broadcasted_iota(jnp.int32, gate.shape, 1).astype(F32)
    sel = jnp.zeros(gate.shape, F32)
    for _ in range(MOBA_TOPK):
        mx = jnp.max(gate, axis=1, keepdims=True)
        is_max = (gate == mx) & (mx > NEG_INF)
        first = jnp.min(jnp.where(is_max, lane, float(LANES)), axis=1, keepdims=True)
        pick = lane == first
        sel = jnp.where(pick, 1.0, sel)
        gate = jnp.where(pick, NEG_INF, gate)
    return sel


def _attn_sample_kernel(pt_ref, q_ref, kn_ref, vn_ref, k0_ref, k1_ref, v0_ref, v1_ref,
                        g_ref, slope_ref, o_ref,
                        dq_ref, mb_ref, mall_ref, lall_ref, gall_ref, op_ref,
                        *, n_heads, tq, n_blocks, past_len, scale):
    del pt_ref
    n = pl.program_id(1)
    rows = n_heads * tq
    page_keys = PAGE_SIZE * n_heads
    slope = slope_ref[...]
    q = q_ref[0]
    qb = (q * scale).astype(BF16)
    lane = lax.broadcasted_iota(jnp.int32, (rows, LANES), 1)

    @pl.when(n == 0)
    def _():
        r = lax.broadcasted_iota(jnp.int32, (rows, page_keys), 0)
        c = lax.broadcasted_iota(jnp.int32, (rows, page_keys), 1)
        dq_ref[...] = (r % tq + past_len - c // n_heads).astype(F32)
        mb_ref[...] = jnp.where(c % n_heads == r // tq, 0.0, jnp.inf)
        mall_ref[...] = jnp.full((rows, LANES), NEG_INF, F32)
        lall_ref[...] = jnp.zeros((rows, LANES), F32)
        gall_ref[...] = jnp.full((rows, LANES), NEG_INF, F32)

    def scores(keys, n_keys, page_pos):
        raw = lax.dot_general(qb, keys.astype(BF16), _NT, preferred_element_type=F32)
        dist = dq_ref[:, 0:n_keys] - page_pos
        return (raw - slope * dist) - mb_ref[:, 0:n_keys], dist

    k0 = k0_ref[...]
    k1 = k1_ref[...]
    pos0 = (n * MOBA_BLOCK).astype(F32)
    s0, _ = scores(k0.reshape(page_keys, HEAD_DIM), page_keys, pos0)
    s1, _ = scores(k1.reshape(page_keys, HEAD_DIM), page_keys, pos0 + float(PAGE_SIZE))
    m = jnp.maximum(jnp.max(s0, axis=1, keepdims=True), jnp.max(s1, axis=1, keepdims=True))
    p0 = jnp.exp(s0 - m)
    p1 = jnp.exp(s1 - m)
    l = jnp.sum(p0, axis=1, keepdims=True) + jnp.sum(p1, axis=1, keepdims=True)
    op_ref[n] = (
        jnp.dot(p0.astype(BF16), v0_ref[...].reshape(page_keys, HEAD_DIM).astype(BF16),
                preferred_element_type=F32)
        + jnp.dot(p1.astype(BF16), v1_ref[...].reshape(page_keys, HEAD_DIM).astype(BF16),
                  preferred_element_type=F32))
    kmean = (jnp.sum(k0, axis=0) + jnp.sum(k1, axis=0)) * (1.0 / MOBA_BLOCK)
    kmean_rows = jnp.concatenate(
        [jnp.broadcast_to(kmean[h:h + 1, :], (tq, HEAD_DIM)) for h in range(n_heads)], axis=0)
    g = jnp.sum(q * kmean_rows, axis=1, keepdims=True)
    mall_ref[...] = jnp.where(lane == n, m, mall_ref[...])
    lall_ref[...] = jnp.where(lane == n, l, lall_ref[...])
    gall_ref[...] = jnp.where(lane == n, g, gall_ref[...])

    @pl.when(n == n_blocks - 1)
    def _():
        sel = _topk_select_lanes(gall_ref[...]) > 0.0
        m_all = jnp.where(sel, mall_ref[...], NEG_INF)
        own_keys = tq * n_heads
        s_own, dist = scores(kn_ref[0], own_keys, float(past_len))
        s_own = jnp.where(dist >= 0.0, s_own, NEG_INF)
        m_tot = jnp.maximum(jnp.max(m_all, axis=1, keepdims=True),
                            jnp.max(s_own, axis=1, keepdims=True))
        w = jnp.exp(m_all - m_tot)
        p_own = jnp.exp(s_own - m_tot)
        l_tot = (jnp.sum(w * lall_ref[...], axis=1, keepdims=True)
                 + jnp.sum(p_own, axis=1, keepdims=True))
        acc = jnp.dot(p_own.astype(BF16), vn_ref[0].astype(BF16), preferred_element_type=F32)
        for i in range(n_blocks):
            acc = acc + w[:, i:i + 1] * op_ref[i]
        o = acc / l_tot
        ms = jnp.mean(o * o, axis=-1, keepdims=True)
        on = o * lax.rsqrt(ms + EPS)
        out = jnp.concatenate(
            [on[h * tq:(h + 1) * tq, :] * g_ref[:, h * HEAD_DIM:(h + 1) * HEAD_DIM]
             for h in range(n_heads)], axis=1)
        o_ref[0] = out.astype(o_ref.dtype)


def _attn_sample(q, k_new, v_new, cache_k, cache_v, layer, page_table, g_ao_l):
    nb, tq, width = q.shape
    n_heads = width // HEAD_DIM
    rows = n_heads * tq
    n_pages = page_table.shape[1]
    past_len = n_pages * PAGE_SIZE
    n_blocks = n_pages // PAGES_PER_BLOCK
    assert rows % SUBLANES == 0 and n_blocks <= LANES
    assert past_len % MOBA_BLOCK == 0 and (past_len + tq - 1) // MOBA_BLOCK == n_blocks
    q_rows = q.reshape(nb, tq, n_heads, HEAD_DIM).transpose(0, 2, 1, 3).reshape(nb, rows, HEAD_DIM)
    kn = k_new.reshape(nb, tq * n_heads, HEAD_DIM)
    vn = v_new.reshape(nb, tq * n_heads, HEAD_DIM)
    slope_col = jnp.repeat(_alibi_slopes(n_heads), tq).reshape(rows, 1)
    pt_flat = page_table.reshape(-1)

    def page_spec(which):
        return pl.BlockSpec(
            (None, None, PAGE_SIZE, n_heads, HEAD_DIM),
            lambda b, n, pt: (layer, pt[b * n_pages + PAGES_PER_BLOCK * n + which], 0, 0, 0))

    seq_spec = pl.BlockSpec((1, rows, HEAD_DIM), lambda b, n, pt: (b, 0, 0))
    page_keys = PAGE_SIZE * n_heads
    grid_spec = pltpu.PrefetchScalarGridSpec(
        num_scalar_prefetch=1,
        grid=(nb, n_blocks),
        in_specs=[
            seq_spec,
            pl.BlockSpec((1, tq * n_heads, HEAD_DIM), lambda b, n, pt: (b, 0, 0)),
            pl.BlockSpec((1, tq * n_heads, HEAD_DIM), lambda b, n, pt: (b, 0, 0)),
            page_spec(0), page_spec(1), page_spec(0), page_spec(1),
            pl.BlockSpec((1, width), lambda b, n, pt: (0, 0)),
            pl.BlockSpec((rows, 1), lambda b, n, pt: (0, 0)),
        ],
        out_specs=pl.BlockSpec((1, tq, width), lambda b, n, pt: (b, 0, 0)),
        scratch_shapes=[
            pltpu.VMEM((rows, page_keys), F32),
            pltpu.VMEM((rows, page_keys), F32),
            pltpu.VMEM((rows, LANES), F32),
            pltpu.VMEM((rows, LANES), F32),
            pltpu.VMEM((rows, LANES), F32),
            pltpu.VMEM((n_blocks, rows, HEAD_DIM), F32),
        ],
    )
    kern = functools.partial(_attn_sample_kernel, n_heads=n_heads, tq=tq, n_blocks=n_blocks,
                             past_len=past_len, scale=HEAD_DIM ** -0.5)
    return pl.pallas_call(
        kern,
        grid_spec=grid_spec,
        out_shape=jax.ShapeDtypeStruct((nb, tq, width), BF16),
        compiler_params=_params("arbitrary", "arbitrary"),
        name="attn_sample",
    )(pt_flat, q_rows, kn, vn, cache_k, cache_k, cache_v, cache_v, g_ao_l.reshape(1, width),
      slope_col)


CONV_COLS_PROMPT = 256
CONV_COLS_SAMPLE = 2048


def _layer(xp, xs, mod_p, mod_s, layer, prev_p, prev_s, cache_k, cache_v, page_table,
           g1, g2, w_in, conv_w, g_ao, g_co, w_out, w_up, w_down):
    bp, tp, d = xp.shape
    bs, ts, _ = xs.shape
    mp, ms = bp * tp, bs * ts
    aw = g_ao.shape[1]
    cd = g_co.shape[1]
    hp = _norm_mod(xp, g1[layer], mod_p, shift_chunk=0, scale_chunk=1).reshape(mp, d)
    hs = _norm_mod(xs, g1[layer], mod_s, shift_chunk=0, scale_chunk=1).reshape(ms, d)

    def in_proj(col0, ncols):
        out_p, out_s = _mm(hp, hs, w_in, layer, col0, ncols, F32)
        return out_p.reshape(bp, tp, ncols), out_s.reshape(bs, ts, ncols)

    qp, qs = in_proj(0, aw)
    kp, ks = in_proj(aw, aw)
    vp, vs = in_proj(2 * aw, aw)
    cvp, cvs = in_proj(3 * aw, 3 * cd)
    attn_p = _attn_prompt(qp, kp, vp, g_ao[layer])
    attn_s = _attn_sample(qs, ks, vs, cache_k, cache_v, layer, page_table, g_ao[layer])
    conv_p, un_p = _short_conv(cvp, prev_p, conv_w[layer], g_co[layer], CONV_COLS_PROMPT)
    conv_s, un_s = _short_conv(cvs, prev_s, conv_w[layer], g_co[layer], CONV_COLS_SAMPLE)
    xp, xs = _out_proj(attn_p.reshape(mp, aw), conv_p.reshape(mp, cd),
                       attn_s.reshape(ms, aw), conv_s.reshape(ms, cd),
                       w_out, layer, xp, mod_p, xs, mod_s, gate_chunk=2)
    h2p = _norm_mod(xp, g2[layer], mod_p, shift_chunk=3, scale_chunk=4).reshape(mp, d)
    h2s = _norm_mod(xs, g2[layer], mod_s, shift_chunk=3, scale_chunk=4).reshape(ms, d)
    hid_p, hid_s = _mm(h2p, h2s, w_up, layer, 0, w_up.shape[2], BF16, relu2=True)
    xp, xs = _down_proj(hid_p, hid_s, w_down, layer, xp, mod_p, xs, mod_s, gate_chunk=5)
    return xp, xs, kp, vp, un_p, ks, vs, un_s


def kernel(x_prompt, x_sample, cache_k, cache_v, state_conv, page_table, c_prompt, c_sample,
           w_ada, b_ada, g_norm1, g_norm2, w_in, conv_w, g_attn_out, g_conv_out,
           w_out, w_up, w_down, g_final):
    depth = w_ada.shape[0]
    bp, tp, d = x_prompt.shape
    bs, ts, _ = x_sample.shape
    aw = g_attn_out.shape[1]
    cd = g_conv_out.shape[1]
    n_heads = aw // HEAD_DIM
    kw = conv_w.shape[1]

    n_c = bp + bs
    c_rows = -(-n_c // SUBLANES) * SUBLANES
    c_all = jnp.concatenate([c_prompt, c_sample, jnp.zeros((c_rows - n_c, d), F32)], axis=0)
    mod = _ada(c_all, w_ada, b_ada)

    u_zero = jnp.zeros((bp, kw - 1, cd), F32)

    weights = (g_norm1, g_norm2, w_in, conv_w, g_attn_out, g_conv_out, w_out, w_up, w_down)
    xp, xs = x_prompt, x_sample
    kp_l, vp_l, cp_l, ks_l, vs_l, cs_l = [], [], [], [], [], []
    for layer in range(depth):
        mod_p = mod[layer, :bp].reshape(bp, 1, -1)
        mod_s = mod[layer, bp:n_c].reshape(bs, 1, -1)
        xp, xs, kp, vp, cp, kn, vn, cn = _layer(
            xp, xs, mod_p, mod_s, layer, u_zero, state_conv[layer], cache_k, cache_v, page_table,
            *weights)
        kp_l.append(kp.reshape(bp, tp, n_heads, HEAD_DIM))
        vp_l.append(vp.reshape(bp, tp, n_heads, HEAD_DIM))
        cp_l.append(cp)
        ks_l.append(kn.reshape(bs, ts, n_heads, HEAD_DIM))
        vs_l.append(vn.reshape(bs, ts, n_heads, HEAD_DIM))
        cs_l.append(cn)
    y_prompt = _final_norm(xp, g_final)
    y_sample = _final_norm(xs, g_final)
    return (y_prompt, y_sample, jnp.stack(kp_l), jnp.stack(vp_l), jnp.stack(cp_l),
            jnp.stack(ks_l), jnp.stack(vs_l), jnp.stack(cs_l))
```

```python
import functools

import jax
import jax.numpy as jnp
from jax import lax
from jax.experimental import pallas as pl
from jax.experimental.pallas import tpu as pltpu

F32 = jnp.float32
BF16 = jnp.bfloat16

HEAD_DIM = 128
PAGE_SIZE = 128
MOBA_BLOCK = 256
MOBA_TOPK = 3
ALIBI_MAX_BIAS = 8.0
EPS = 1e-6
LOG2_E = 1.4426950408889634
PAGES_PER_BLOCK = MOBA_BLOCK // PAGE_SIZE
BLOCKS_PER_STEP = 2

V7X_VMEM_BYTES = 64 * 2**20
VMEM_LIMIT_BYTES = V7X_VMEM_BYTES - 8 * 2**20
LANES = 128
SUBLANES = 8

NEG_INF = float("-inf")
_NT = (((1,), (1,)), ((), ()))


def _params(*sem):
    return pltpu.CompilerParams(dimension_semantics=sem, vmem_limit_bytes=VMEM_LIMIT_BYTES)


def _ada_kernel(c_ref, w_ref, b_ref, o_ref):
    c = c_ref[...]
    s = c * jax.nn.sigmoid(c)
    o_ref[...] = jnp.dot(s.astype(BF16), w_ref[...].astype(BF16),
                         preferred_element_type=F32) + b_ref[...]


def _ada(c_all, w_ada, b_ada, tn=512):
    depth, d, cols = w_ada.shape
    rows = c_all.shape[0]
    return pl.pallas_call(
        _ada_kernel,
        grid=(depth, cols // tn),
        in_specs=[
            pl.BlockSpec((rows, d), lambda l, j: (0, 0)),
            pl.BlockSpec((None, d, tn), lambda l, j: (l, 0, j)),
            pl.BlockSpec((None, 1, tn), lambda l, j: (l, 0, j)),
        ],
        out_specs=pl.BlockSpec((None, rows, tn), lambda l, j: (l, 0, j)),
        out_shape=jax.ShapeDtypeStruct((depth, rows, cols), F32),
        compiler_params=_params("arbitrary", "arbitrary"),
        name="ada_mod",
    )(c_all, w_ada, b_ada.reshape(depth, 1, cols))


def _norm_mod_kernel(x_ref, g_ref, sc_ref, sh_ref, o_ref):
    x = x_ref[...]
    ms = jnp.mean(x * x, axis=-1, keepdims=True)
    xn = x * lax.rsqrt(ms + EPS) * g_ref[...]
    o_ref[...] = (xn * (1.0 + sc_ref[...]) + sh_ref[...]).astype(o_ref.dtype)


def _norm_kernel(x_ref, g_ref, o_ref):
    x = x_ref[...]
    ms = jnp.mean(x * x, axis=-1, keepdims=True)
    o_ref[...] = (x * lax.rsqrt(ms + EPS) * g_ref[...]).astype(o_ref.dtype)


def _row_tiles(nb_total, t_total, max_rows):
    if t_total >= max_rows:
        return 1, max_rows
    assert nb_total * t_total <= max_rows
    return nb_total, t_total


def _norm_mod(x, g, mod, shift_chunk, scale_chunk, max_rows=256):
    nbt, t, d = x.shape
    nb, tt = _row_tiles(nbt, t, max_rows)
    tpb = t // tt
    return pl.pallas_call(
        _norm_mod_kernel,
        grid=(nbt // nb, tpb),
        in_specs=[
            pl.BlockSpec((nb, tt, d), lambda b, i: (b, i, 0)),
            pl.BlockSpec((1, d), lambda b, i: (0, 0)),
            pl.BlockSpec((nb, 1, d), lambda b, i: (b, 0, scale_chunk)),
            pl.BlockSpec((nb, 1, d), lambda b, i: (b, 0, shift_chunk)),
        ],
        out_specs=pl.BlockSpec((nb, tt, d), lambda b, i: (b, i, 0)),
        out_shape=jax.ShapeDtypeStruct(x.shape, BF16),
        compiler_params=_params("arbitrary", "arbitrary"),
        name="norm_mod",
    )(x, g.reshape(1, d), mod, mod)


def _final_norm(x, g, max_rows=256):
    nbt, t, d = x.shape
    nb, tt = _row_tiles(nbt, t, max_rows)
    return pl.pallas_call(
        _norm_kernel,
        grid=(nbt // nb, t // tt),
        in_specs=[
            pl.BlockSpec((nb, tt, d), lambda b, i: (b, i, 0)),
            pl.BlockSpec((1, d), lambda b, i: (0, 0)),
        ],
        out_specs=pl.BlockSpec((nb, tt, d), lambda b, i: (b, i, 0)),
        out_shape=jax.ShapeDtypeStruct(x.shape, x.dtype),
        compiler_params=_params("arbitrary", "arbitrary"),
        name="final_norm",
    )(x, g.reshape(1, d))


MM_ROWS = 1024
MM_COLS = 512


def _sample_col(i, j, n_col_tiles):
    return jnp.where(i == 0, j, n_col_tiles - 1)


def _mm_kernel(x_ref, xs_ref, w_ref, o_ref, os_ref, *, relu2):
    w = w_ref[...].astype(BF16)

    def project(x):
        acc = jnp.dot(x, w, preferred_element_type=F32)
        return jnp.square(jnp.maximum(acc, 0.0)) if relu2 else acc

    o_ref[...] = project(x_ref[...]).astype(o_ref.dtype)

    @pl.when(pl.program_id(0) == 0)
    def _():
        os_ref[...] = project(xs_ref[...]).astype(os_ref.dtype)


def _mm(x2d, xs2d, w, layer, col0, ncols, out_dtype, relu2=False):
    m, k = x2d.shape
    ms = xs2d.shape[0]
    tm, tn = MM_ROWS, MM_COLS
    nj = ncols // tn
    off = col0 // tn
    return pl.pallas_call(
        functools.partial(_mm_kernel, relu2=relu2),
        grid=(m // tm, nj),
        in_specs=[
            pl.BlockSpec((tm, k), lambda i, j: (i, 0)),
            pl.BlockSpec((ms, k), lambda i, j: (0, 0)),
            pl.BlockSpec((None, k, tn), lambda i, j: (layer, 0, off + j)),
        ],
        out_specs=[
            pl.BlockSpec((tm, tn), lambda i, j: (i, j)),
            pl.BlockSpec((ms, tn), lambda i, j: (0, _sample_col(i, j, nj))),
        ],
        out_shape=[
            jax.ShapeDtypeStruct((m, ncols), out_dtype),
            jax.ShapeDtypeStruct((ms, ncols), out_dtype),
        ],
        compiler_params=_params("arbitrary", "arbitrary"),
        name="mm",
    )(x2d, xs2d, w)


def _mm_res2_kernel(a_ref, c_ref, as_ref, cs_ref, wa_ref, wc_ref, x_ref, g_ref, xs_ref, gs_ref,
                    o_ref, os_ref):
    wa = wa_ref[...].astype(BF16)
    wc = wc_ref[...].astype(BF16)

    def gated_residual(a, c, x, g):
        acc = (jnp.dot(a, wa, preferred_element_type=F32)
               + jnp.dot(c, wc, preferred_element_type=F32))
        return x + g * acc.reshape(x.shape)

    o_ref[...] = gated_residual(a_ref[...], c_ref[...], x_ref[...], g_ref[...])

    @pl.when(pl.program_id(0) == 0)
    def _():
        os_ref[...] = gated_residual(as_ref[...], cs_ref[...], xs_ref[...], gs_ref[...])


def _out_proj(a2d, c2d, as2d, cs2d, w_out, layer, x, mod, xs, mod_s, gate_chunk):
    nbt, t, d = x.shape
    nbs, ts, _ = xs.shape
    ka = a2d.shape[1]
    tm, tn = MM_ROWS, MM_COLS
    ms = nbs * ts
    tpb = t // tm
    nj = d // tn
    gate_off = gate_chunk * nj

    def sample_block(shape):
        return pl.BlockSpec(shape, lambda i, j: (0, 0, _sample_col(i, j, nj)))

    return pl.pallas_call(
        _mm_res2_kernel,
        grid=(a2d.shape[0] // tm, nj),
        in_specs=[
            pl.BlockSpec((tm, ka), lambda i, j: (i, 0)),
            pl.BlockSpec((tm, ka), lambda i, j: (i, 0)),
            pl.BlockSpec((ms, ka), lambda i, j: (0, 0)),
            pl.BlockSpec((ms, ka), lambda i, j: (0, 0)),
            pl.BlockSpec((None, ka, tn), lambda i, j: (layer, 0, j)),
            pl.BlockSpec((None, ka, tn), lambda i, j: (layer, 1, j)),
            pl.BlockSpec((1, tm, tn), lambda i, j: (i // tpb, i % tpb, j)),
            pl.BlockSpec((1, 1, tn), lambda i, j: (i // tpb, 0, gate_off + j)),
            sample_block((nbs, ts, tn)),
            pl.BlockSpec((nbs, 1, tn), lambda i, j: (0, 0, gate_off + _sample_col(i, j, nj))),
        ],
        out_specs=[
            pl.BlockSpec((1, tm, tn), lambda i, j: (i // tpb, i % tpb, j)),
            sample_block((nbs, ts, tn)),
        ],
        out_shape=[
            jax.ShapeDtypeStruct(x.shape, x.dtype),
            jax.ShapeDtypeStruct(xs.shape, xs.dtype),
        ],
        compiler_params=_params("arbitrary", "arbitrary"),
        name="out_proj",
    )(a2d, c2d, as2d, cs2d, w_out, w_out, x, mod, xs, mod_s)


DOWN_ROWS = 2048
DOWN_COLS = 1024
DOWN_K = 1024
DOWN_COL_CHUNK = 512


def _mm_res_k_kernel(h_ref, hs_ref, w_ref, x_ref, g_ref, xs_ref, gs_ref, o_ref, os_ref):
    i = pl.program_id(0)
    k = pl.program_id(2)
    last = pl.num_programs(2) - 1
    col_chunks = [slice(c, c + DOWN_COL_CHUNK) for c in range(0, w_ref.shape[1], DOWN_COL_CHUNK)]

    def accumulate(acc_ref, h):
        rows = acc_ref.shape[:-1]
        for cols in col_chunks:
            part = jnp.dot(h, w_ref[:, cols].astype(BF16), preferred_element_type=F32)
            acc_ref[:, :, cols] += part.reshape(*rows, DOWN_COL_CHUNK)

    @pl.when(k == 0)
    def _():
        o_ref[...] = jnp.zeros_like(o_ref)

    accumulate(o_ref, h_ref[...])

    @pl.when(k == last)
    def _():
        o_ref[...] = x_ref[...] + g_ref[...] * o_ref[...]

    @pl.when((i == 0) & (k == 0))
    def _():
        os_ref[...] = jnp.zeros_like(os_ref)

    @pl.when(i == 0)
    def _():
        accumulate(os_ref, hs_ref[...])

    @pl.when((i == 0) & (k == last))
    def _():
        os_ref[...] = xs_ref[...] + gs_ref[...] * os_ref[...]


def _down_proj(h2d, hs2d, w_down, layer, x, mod, xs, mod_s, gate_chunk):
    nbt, t, d = x.shape
    nbs, ts, _ = xs.shape
    kdim = h2d.shape[1]
    tm, tn, tk = DOWN_ROWS, DOWN_COLS, DOWN_K
    ms = nbs * ts
    tpb = t // tm
    nj = d // tn
    nk = kdim // tk
    gate_off = gate_chunk * nj

    def sample_block(shape):
        return pl.BlockSpec(shape, lambda i, j, k: (0, 0, _sample_col(i, j, nj)))

    return pl.pallas_call(
        _mm_res_k_kernel,
        grid=(h2d.shape[0] // tm, nj, nk),
        in_specs=[
            pl.BlockSpec((tm, tk), lambda i, j, k: (i, k)),
            pl.BlockSpec((ms, tk), lambda i, j, k: (0, _sample_col(i, k, nk))),
            pl.BlockSpec((None, tk, tn), lambda i, j, k: (layer, k, j)),
            pl.BlockSpec((1, tm, tn), lambda i, j, k: (i // tpb, i % tpb, j),
                         pipeline_mode=pl.Buffered(1)),
            pl.BlockSpec((1, 1, tn), lambda i, j, k: (i // tpb, 0, gate_off + j)),
            sample_block((nbs, ts, tn)),
            pl.BlockSpec((nbs, 1, tn), lambda i, j, k: (0, 0, gate_off + _sample_col(i, j, nj))),
        ],
        out_specs=[
            pl.BlockSpec((1, tm, tn), lambda i, j, k: (i // tpb, i % tpb, j)),
            sample_block((nbs, ts, tn)),
        ],
        out_shape=[
            jax.ShapeDtypeStruct(x.shape, x.dtype),
            jax.ShapeDtypeStruct(xs.shape, xs.dtype),
        ],
        compiler_params=_params("arbitrary", "arbitrary", "arbitrary"),
        name="down_proj",
    )(h2d, hs2d, w_down, x, mod, xs, mod_s)


_PAD_ROWS = SUBLANES


def _conv_kernel(gb_ref, gc_ref, u_ref, prev_ref, w_ref, g_ref, o_ref, un_ref, pad_ref, *, t, cw, kw):
    u = gc_ref[0] * u_ref[0]
    pad_ref[pl.ds(_PAD_ROWS - (kw - 1), kw - 1), :] = prev_ref[0]
    pad_ref[pl.ds(_PAD_ROWS, t), :] = u
    w = w_ref[...]
    y = None
    for i in range(kw):
        tap = u if i == kw - 1 else pad_ref[pl.ds(_PAD_ROWS - (kw - 1) + i, t), :]
        term = w[i:i + 1, :] * tap
        y = term if y is None else y + term
    co = gb_ref[0] * y
    outs = []
    for gi in range(cw // HEAD_DIM):
        sl = slice(gi * HEAD_DIM, (gi + 1) * HEAD_DIM)
        c = co[:, sl]
        ms = jnp.mean(c * c, axis=-1, keepdims=True)
        outs.append(c * lax.rsqrt(ms + EPS) * g_ref[:, sl])
    o_ref[0] = jnp.concatenate(outs, axis=-1).astype(o_ref.dtype)
    un_ref[0] = pad_ref[pl.ds(_PAD_ROWS + t - (kw - 1), kw - 1), :]


def _short_conv(cv, prev, conv_w_l, g_co_l, cw):
    nb, t, _ = cv.shape
    kw, c = conv_w_l.shape
    ncb = c // cw
    return pl.pallas_call(
        functools.partial(_conv_kernel, t=t, cw=cw, kw=kw),
        grid=(nb, ncb),
        in_specs=[
            pl.BlockSpec((1, t, cw), lambda b, j: (b, 0, j)),
            pl.BlockSpec((1, t, cw), lambda b, j: (b, 0, ncb + j)),
            pl.BlockSpec((1, t, cw), lambda b, j: (b, 0, 2 * ncb + j)),
            pl.BlockSpec((1, kw - 1, cw), lambda b, j: (b, 0, j)),
            pl.BlockSpec((kw, cw), lambda b, j: (0, j)),
            pl.BlockSpec((1, cw), lambda b, j: (0, j)),
        ],
        out_specs=[
            pl.BlockSpec((1, t, cw), lambda b, j: (b, 0, j)),
            pl.BlockSpec((1, kw - 1, cw), lambda b, j: (b, 0, j)),
        ],
        out_shape=[
            jax.ShapeDtypeStruct((nb, t, c), BF16),
            jax.ShapeDtypeStruct((nb, kw - 1, c), F32),
        ],
        scratch_shapes=[pltpu.VMEM((t + _PAD_ROWS, cw), F32)],
        compiler_params=_params("arbitrary", "arbitrary"),
        name="short_conv",
    )(cv, cv, cv, prev, conv_w_l, g_co_l.reshape(1, c))


def _topk_select(gate, n_blocks):
    bidx = lax.broadcasted_iota(jnp.int32, gate.shape, 0)
    sel = jnp.zeros(gate.shape, F32)
    for n in range(n_blocks):
        row = gate[n:n + 1, :]
        beats = (gate > row) | ((gate == row) & (bidx < n))
        cnt = jnp.sum(beats.astype(F32), axis=0, keepdims=True)
        keep = jnp.where((cnt < MOBA_TOPK) & (row > NEG_INF), 1.0, 0.0)
        sel = jnp.where(bidx == n, keep, sel)
    return sel


def _attn_prompt_kernel(q_ref, k_ref, v_ref, g_ref, slope_ref, o_ref,
                        kb_ref, vt_ref, s_ref, p_ref, *, n_blocks, scale):
    blk = MOBA_BLOCK
    k = k_ref[0]
    kb_ref[...] = k.astype(BF16)
    vt_ref[...] = v_ref[0].T.astype(BF16)
    kmean = jnp.mean(k.reshape(n_blocks, blk, HEAD_DIM), axis=1)
    q = q_ref[0]
    gate = lax.dot_general(kmean, q, _NT, precision=lax.Precision.HIGHEST,
                           preferred_element_type=F32)
    bidx = lax.broadcasted_iota(jnp.int32, gate.shape, 0)
    qblk = lax.broadcasted_iota(jnp.int32, gate.shape, 1) // blk
    gate = jnp.where(bidx < qblk, gate, NEG_INF)
    sel = _topk_select(gate, n_blocks)

    qs = (q * (scale * LOG2_E)).astype(BF16)
    slope = slope_ref[0] * LOG2_E
    base = (lax.broadcasted_iota(jnp.int32, (blk, blk), 1)
            - lax.broadcasted_iota(jnp.int32, (blk, blk), 0))
    bias = slope * base.astype(F32)
    bias_own = jnp.where(base >= 0, bias, jnp.inf)

    for qi in range(n_blocks):
        cols = slice(qi * blk, (qi + 1) * blk)
        n_keys = (qi + 1) * blk
        slot = qi % 2
        raw = lax.dot_general(kb_ref[0:n_keys, :], qs[cols, :], _NT,
                              preferred_element_type=F32)
        m = None
        for n in range(qi + 1):
            rows = slice(n * blk, (n + 1) * blk)
            if n == qi:
                s = raw[rows, :] - bias_own
            else:
                row_bias = jnp.where(sel[n:n + 1, cols] > 0.0, slope * float((qi - n) * blk), jnp.inf)
                s = (raw[rows, :] - bias) - row_bias
            s_ref[slot, rows, :] = s
            cm = jnp.max(s, axis=0, keepdims=True)
            m = cm if m is None else jnp.maximum(m, cm)
        l = jnp.zeros_like(m)
        for n in range(qi + 1):
            rows = slice(n * blk, (n + 1) * blk)
            p = jnp.exp2(s_ref[slot, rows, :] - m)
            l = l + jnp.sum(p, axis=0, keepdims=True)
            p_ref[slot, rows, :] = p.astype(BF16)
        o_t = jnp.dot(vt_ref[:, 0:n_keys], p_ref[slot, 0:n_keys, :],
                      preferred_element_type=F32)
        o = (o_t / l).T
        ms = jnp.mean(o * o, axis=-1, keepdims=True)
        o_ref[0, cols, :] = (o * lax.rsqrt(ms + EPS) * g_ref[...]).astype(o_ref.dtype)


def _alibi_slopes(n_heads):
    return 2.0 ** (-ALIBI_MAX_BIAS * jnp.arange(1, n_heads + 1, dtype=F32) / n_heads)


def _attn_prompt(q, k, v, g_ao_l):
    nb, t, width = q.shape
    n_heads = width // HEAD_DIM
    n_blocks = t // MOBA_BLOCK
    slopes = jnp.broadcast_to(_alibi_slopes(n_heads)[:, None, None], (n_heads, 1, MOBA_BLOCK))
    kern = functools.partial(_attn_prompt_kernel, n_blocks=n_blocks, scale=HEAD_DIM ** -0.5)
    head_spec = pl.BlockSpec((1, t, HEAD_DIM), lambda b, h: (b, 0, h))
    return pl.pallas_call(
        kern,
        grid=(nb, n_heads),
        in_specs=[
            head_spec, head_spec, head_spec,
            pl.BlockSpec((1, HEAD_DIM), lambda b, h: (0, h)),
            pl.BlockSpec((1, 1, MOBA_BLOCK), lambda b, h: (h, 0, 0)),
        ],
        out_specs=head_spec,
        out_shape=jax.ShapeDtypeStruct((nb, t, width), BF16),
        scratch_shapes=[
            pltpu.VMEM((t, HEAD_DIM), BF16),
            pltpu.VMEM((HEAD_DIM, t), BF16),
            pltpu.VMEM((2, t, MOBA_BLOCK), F32),
            pltpu.VMEM((2, t, MOBA_BLOCK), BF16),
        ],
        compiler_params=_params("arbitrary", "arbitrary"),
        name="attn_prompt",
    )(q, k, v, g_ao_l.reshape(1, width), slopes)


def _topk_select_lanes(gate):
    lane = lax.broadcasted_iota(jnp.int32, gate.shape, 1).astype(F32)
    sel = jnp.zeros(gate.shape, F32)
    for _ in range(MOBA_TOPK):
        mx = jnp.max(gate, axis=1, keepdims=True)
        is_max = (gate == mx) & (mx > NEG_INF)
        first = jnp.min(jnp.where(is_max, lane, float(LANES)), axis=1, keepdims=True)
        pick = lane == first
        sel = jnp.where(pick, 1.0, sel)
        gate = jnp.where(pick, NEG_INF, gate)
    return sel


def _attn_sample_kernel(pt_ref, q_ref, kn_ref, vn_ref, *rest, n_heads, tq, n_blocks, past_len, scale):
    n_page_refs = BLOCKS_PER_STEP * PAGES_PER_BLOCK
    k_refs = rest[:n_page_refs]
    v_refs = rest[n_page_refs:2 * n_page_refs]
    (g_ref, slope_ref, o_ref,
     dq_ref, mb_ref, mall_ref, lall_ref, gall_ref, op_ref) = rest[2 * n_page_refs:]
    del pt_ref
    step = pl.program_id(1)
    rows = n_heads * tq
    page_keys = PAGE_SIZE * n_heads
    slope = slope_ref[...]
    q = q_ref[0]
    qb = (q * scale).astype(BF16)
    lane = lax.broadcasted_iota(jnp.int32, (rows, LANES), 1)

    @pl.when(step == 0)
    def _():
        r = lax.broadcasted_iota(jnp.int32, (rows, page_keys), 0)
        c = lax.broadcasted_iota(jnp.int32, (rows, page_keys), 1)
        dq_ref[...] = (r % tq + past_len - c // n_heads).astype(F32)
        mb_ref[...] = jnp.where(c % n_heads == r // tq, 0.0, jnp.inf)
        mall_ref[...] = jnp.full((rows, LANES), NEG_INF, F32)
        lall_ref[...] = jnp.zeros((rows, LANES), F32)
        gall_ref[...] = jnp.full((rows, LANES), NEG_INF, F32)

    def scores(keys, n_keys, page_pos):
        raw = lax.dot_general(qb, keys.astype(BF16), _NT, preferred_element_type=F32)
        dist = dq_ref[:, 0:n_keys] - page_pos
        return (raw - slope * dist) - mb_ref[:, 0:n_keys], dist

    def block_partial(n, k0_ref, k1_ref, v0_ref, v1_ref):
        k0 = k0_ref[...]
        k1 = k1_ref[...]
        pos0 = (n * MOBA_BLOCK).astype(F32)
        s0, _ = scores(k0.reshape(page_keys, HEAD_DIM), page_keys, pos0)
        s1, _ = scores(k1.reshape(page_keys, HEAD_DIM), page_keys, pos0 + float(PAGE_SIZE))
        m = jnp.maximum(jnp.max(s0, axis=1, keepdims=True), jnp.max(s1, axis=1, keepdims=True))
        p0 = jnp.exp(s0 - m)
        p1 = jnp.exp(s1 - m)
        l = jnp.sum(p0, axis=1, keepdims=True) + jnp.sum(p1, axis=1, keepdims=True)
        op_ref[n] = (
            jnp.dot(p0.astype(BF16), v0_ref[...].reshape(page_keys, HEAD_DIM).astype(BF16),
                    preferred_element_type=F32)
            + jnp.dot(p1.astype(BF16), v1_ref[...].reshape(page_keys, HEAD_DIM).astype(BF16),
                      preferred_element_type=F32))
        kmean = (jnp.sum(k0, axis=0) + jnp.sum(k1, axis=0)) * (1.0 / MOBA_BLOCK)
        kmean_rows = jnp.concatenate(
            [jnp.broadcast_to(kmean[h:h + 1, :], (tq, HEAD_DIM)) for h in range(n_heads)], axis=0)
        g = jnp.sum(q * kmean_rows, axis=1, keepdims=True)
        mall_ref[...] = jnp.where(lane == n, m, mall_ref[...])
        lall_ref[...] = jnp.where(lane == n, l, lall_ref[...])
        gall_ref[...] = jnp.where(lane == n, g, gall_ref[...])

    for b in range(BLOCKS_PER_STEP):
        pages = slice(b * PAGES_PER_BLOCK, (b + 1) * PAGES_PER_BLOCK)
        block_partial(step * BLOCKS_PER_STEP + b, *k_refs[pages], *v_refs[pages])

    @pl.when(step == pl.num_programs(1) - 1)
    def _():
        sel = _topk_select_lanes(gall_ref[...]) > 0.0
        m_all = jnp.where(sel, mall_ref[...], NEG_INF)
        own_keys = tq * n_heads
        s_own, dist = scores(kn_ref[0], own_keys, float(past_len))
        s_own = jnp.where(dist >= 0.0, s_own, NEG_INF)
        m_tot = jnp.maximum(jnp.max(m_all, axis=1, keepdims=True),
                            jnp.max(s_own, axis=1, keepdims=True))
        w = jnp.exp(m_all - m_tot)
        p_own = jnp.exp(s_own - m_tot)
        l_tot = (jnp.sum(w * lall_ref[...], axis=1, keepdims=True)
                 + jnp.sum(p_own, axis=1, keepdims=True))
        acc = jnp.dot(p_own.astype(BF16), vn_ref[0].astype(BF16), preferred_element_type=F32)
        for i in range(n_blocks):
            acc = acc + w[:, i:i + 1] * op_ref[i]
        o = acc / l_tot
        ms = jnp.mean(o * o, axis=-1, keepdims=True)
        on = o * lax.rsqrt(ms + EPS)
        out = jnp.concatenate(
            [on[h * tq:(h + 1) * tq, :] * g_ref[:, h * HEAD_DIM:(h + 1) * HEAD_DIM]
             for h in range(n_heads)], axis=1)
        o_ref[0] = out.astype(o_ref.dtype)


def _attn_sample(q, k_new, v_new, cache_k, cache_v, layer, page_table, g_ao_l):
    nb, tq, width = q.shape
    n_heads = width // HEAD_DIM
    rows = n_heads * tq
    n_pages = page_table.shape[1]
    past_len = n_pages * PAGE_SIZE
    n_blocks = n_pages // PAGES_PER_BLOCK
    assert rows % SUBLANES == 0 and n_blocks <= LANES
    assert past_len % MOBA_BLOCK == 0 and (past_len + tq - 1) // MOBA_BLOCK == n_blocks
    q_rows = q.reshape(nb, tq, n_heads, HEAD_DIM).transpose(0, 2, 1, 3).reshape(nb, rows, HEAD_DIM)
    kn = k_new.reshape(nb, tq * n_heads, HEAD_DIM)
    vn = v_new.reshape(nb, tq * n_heads, HEAD_DIM)
    slope_col = jnp.repeat(_alibi_slopes(n_heads), tq).reshape(rows, 1)
    pt_flat = page_table.reshape(-1)

    pages_per_step = BLOCKS_PER_STEP * PAGES_PER_BLOCK
    assert n_pages % pages_per_step == 0

    def page_spec(which):
        return pl.BlockSpec(
            (None, None, PAGE_SIZE, n_heads, HEAD_DIM),
            lambda b, n, pt: (layer, pt[b * n_pages + pages_per_step * n + which], 0, 0, 0))

    page_specs = [page_spec(which) for which in range(pages_per_step)]
    seq_spec = pl.BlockSpec((1, rows, HEAD_DIM), lambda b, n, pt: (b, 0, 0))
    page_keys = PAGE_SIZE * n_heads
    grid_spec = pltpu.PrefetchScalarGridSpec(
        num_scalar_prefetch=1,
        grid=(nb, n_pages // pages_per_step),
        in_specs=[
            seq_spec,
            pl.BlockSpec((1, tq * n_heads, HEAD_DIM), lambda b, n, pt: (b, 0, 0)),
            pl.BlockSpec((1, tq * n_heads, HEAD_DIM), lambda b, n, pt: (b, 0, 0)),
            *page_specs, *page_specs,
            pl.BlockSpec((1, width), lambda b, n, pt: (0, 0)),
            pl.BlockSpec((rows, 1), lambda b, n, pt: (0, 0)),
        ],
        out_specs=pl.BlockSpec((1, tq, width), lambda b, n, pt: (b, 0, 0)),
        scratch_shapes=[
            pltpu.VMEM((rows, page_keys), F32),
            pltpu.VMEM((rows, page_keys), F32),
            pltpu.VMEM((rows, LANES), F32),
            pltpu.VMEM((rows, LANES), F32),
            pltpu.VMEM((rows, LANES), F32),
            pltpu.VMEM((n_blocks, rows, HEAD_DIM), F32),
        ],
    )
    kern = functools.partial(_attn_sample_kernel, n_heads=n_heads, tq=tq, n_blocks=n_blocks,
                             past_len=past_len, scale=HEAD_DIM ** -0.5)
    return pl.pallas_call(
        kern,
        grid_spec=grid_spec,
        out_shape=jax.ShapeDtypeStruct((nb, tq, width), BF16),
        compiler_params=_params("arbitrary", "arbitrary"),
        name="attn_sample",
    )(pt_flat, q_rows, kn, vn, *([cache_k] * pages_per_step), *([cache_v] * pages_per_step),
      g_ao_l.reshape(1, width), slope_col)


CONV_COLS_PROMPT = 256
CONV_COLS_SAMPLE = 2048


def _layer(xp, xs, mod_p, mod_s, layer, prev_p, prev_s, cache_k, cache_v, page_table,
           g1, g2, w_in, conv_w, g_ao, g_co, w_out, w_up, w_down):
    bp, tp, d = xp.shape
    bs, ts, _ = xs.shape
    mp, ms = bp * tp, bs * ts
    aw = g_ao.shape[1]
    cd = g_co.shape[1]
    hp = _norm_mod(xp, g1[layer], mod_p, shift_chunk=0, scale_chunk=1).reshape(mp, d)
    hs = _norm_mod(xs, g1[layer], mod_s, shift_chunk=0, scale_chunk=1).reshape(ms, d)

    def in_proj(col0, ncols):
        out_p, out_s = _mm(hp, hs, w_in, layer, col0, ncols, F32)
        return out_p.reshape(bp, tp, ncols), out_s.reshape(bs, ts, ncols)

    qp, qs = in_proj(0, aw)
    kp, ks = in_proj(aw, aw)
    vp, vs = in_proj(2 * aw, aw)
    cvp, cvs = in_proj(3 * aw, 3 * cd)
    attn_p = _attn_prompt(qp, kp, vp, g_ao[layer])
    attn_s = _attn_sample(qs, ks, vs, cache_k, cache_v, layer, page_table, g_ao[layer])
    conv_p, un_p = _short_conv(cvp, prev_p, conv_w[layer], g_co[layer], CONV_COLS_PROMPT)
    conv_s, un_s = _short_conv(cvs, prev_s, conv_w[layer], g_co[layer], CONV_COLS_SAMPLE)
    xp, xs = _out_proj(attn_p.reshape(mp, aw), conv_p.reshape(mp, cd),
                       attn_s.reshape(ms, aw), conv_s.reshape(ms, cd),
                       w_out, layer, xp, mod_p, xs, mod_s, gate_chunk=2)
    h2p = _norm_mod(xp, g2[layer], mod_p, shift_chunk=3, scale_chunk=4).reshape(mp, d)
    h2s = _norm_mod(xs, g2[layer], mod_s, shift_chunk=3, scale_chunk=4).reshape(ms, d)
    hid_p, hid_s = _mm(h2p, h2s, w_up, layer, 0, w_up.shape[2], BF16, relu2=True)
    xp, xs = _down_proj(hid_p, hid_s, w_down, layer, xp, mod_p, xs, mod_s, gate_chunk=5)
    return xp, xs, kp, vp, un_p, ks, vs, un_s


def kernel(x_prompt, x_sample, cache_k, cache_v, state_conv, page_table, c_prompt, c_sample,
           w_ada, b_ada, g_norm1, g_norm2, w_in, conv_w, g_attn_out, g_conv_out,
           w_out, w_up, w_down, g_final):
    depth = w_ada.shape[0]
    bp, tp, d = x_prompt.shape
    bs, ts, _ = x_sample.shape
    aw = g_attn_out.shape[1]
    cd = g_conv_out.shape[1]
    n_heads = aw // HEAD_DIM
    kw = conv_w.shape[1]

    n_c = bp + bs
    c_rows = -(-n_c // SUBLANES) * SUBLANES
    c_all = jnp.concatenate([c_prompt, c_sample, jnp.zeros((c_rows - n_c, d), F32)], axis=0)
    mod = _ada(c_all, w_ada, b_ada)

    u_zero = jnp.zeros((bp, kw - 1, cd), F32)

    weights = (g_norm1, g_norm2, w_in, conv_w, g_attn_out, g_conv_out, w_out, w_up, w_down)
    xp, xs = x_prompt, x_sample
    kp_l, vp_l, cp_l, ks_l, vs_l, cs_l = [], [], [], [], [], []
    for layer in range(depth):
        mod_p = mod[layer, :bp].reshape(bp, 1, -1)
        mod_s = mod[layer, bp:n_c].reshape(bs, 1, -1)
        xp, xs, kp, vp, cp, kn, vn, cn = _layer(
            xp, xs, mod_p, mod_s, layer, u_zero, state_conv[layer], cache_k, cache_v, page_table,
            *weights)
        kp_l.append(kp.reshape(bp, tp, n_heads, HEAD_DIM))
        vp_l.append(vp.reshape(bp, tp, n_heads, HEAD_DIM))
        cp_l.append(cp)
        ks_l.append(kn.reshape(bs, ts, n_heads, HEAD_DIM))
        vs_l.append(vn.reshape(bs, ts, n_heads, HEAD_DIM))
        cs_l.append(cn)
    y_prompt = _final_norm(xp, g_final)
    y_sample = _final_norm(xs, g_final)
    return (y_prompt, y_sample, jnp.stack(kp_l), jnp.stack(vp_l), jnp.stack(cp_l),
            jnp.stack(ks_l), jnp.stack(vs_l), jnp.stack(cs_l))
```
